```python
import jax, jax.numpy as jnp
from jax import lax
import numpy as np

D_MODEL = 1024
BATCH = 8
SEQ = 2048
DEPTH = 2

MEM_LEN = 256
MEM_HEADS = 4
MEM_HEAD_DIM = D_MODEL // MEM_HEADS
D_FF = 2816
POOL_WINDOWS = (2, 4, 8, 16)
POOL_WIDTH = 512
POOL_GROUP = POOL_WIDTH // 4
NSA_HEADS = 8
NSA_KV_HEADS = 2
NSA_HPG = NSA_HEADS // NSA_KV_HEADS
NSA_HEAD_DIM = 64
NSA_WIDTH = NSA_HEADS * NSA_HEAD_DIM
NSA_KV_WIDTH = NSA_KV_HEADS * NSA_HEAD_DIM
CMP_BLOCK = 32
CMP_STRIDE = 16
CMP_HIDDEN = 256
SEL_BLOCK = 64
SEL_TOPK = 8
WINDOW = 256
Q_BLOCK = 128
GMLP_WIDTH = 512
GMLP_GROUPS = 4
GMLP_GROUP_DIM = GMLP_WIDTH // GMLP_GROUPS
GMLP_CHUNK = 128
IN_SIZES = (POOL_WIDTH, NSA_WIDTH, 6 * NSA_KV_WIDTH, 3 * NSA_HEADS, 2 * GMLP_WIDTH, 3 * D_MODEL)
IN_COLS = POOL_WIDTH + NSA_WIDTH + 6 * NSA_KV_WIDTH + 3 * NSA_HEADS + 2 * GMLP_WIDTH + 3 * D_MODEL
EPS = 1e-6
NEG = -1e30

kernel_name = "hybrid_pool_nsa_gmlp_macaron_block"


def rmsnorm(x, g):
    xf = x.astype(jnp.float32)
    y = xf * lax.rsqrt(jnp.mean(xf * xf, axis=-1, keepdims=True) + EPS)
    return (y * g.astype(jnp.float32)).astype(x.dtype)


def layernorm(x, g, b):
    xf = x.astype(jnp.float32)
    mu = jnp.mean(xf, axis=-1, keepdims=True)
    var = jnp.mean(jnp.square(xf - mu), axis=-1, keepdims=True)
    y = (xf - mu) * lax.rsqrt(var + EPS) * g.astype(jnp.float32) + b.astype(jnp.float32)
    return y.astype(x.dtype)


def masked_softmax(s, valid):
    s = jnp.where(valid, s.astype(jnp.float32), NEG)
    return jnp.where(valid, jax.nn.softmax(s, axis=-1), 0.0)


def swiglu(h, w1, w3, w2):
    return (jax.nn.silu(h @ w1) * (h @ w3)) @ w2


def pool_mixer(a, pool_w, pool_scale):
    B, S, _ = a.shape
    csum = jnp.cumsum(a.astype(jnp.float32), axis=1)
    count = jnp.arange(1, S + 1, dtype=jnp.float32)[:, None]
    means = []
    for gi, w in enumerate(POOL_WINDOWS):
        c = csum[..., gi * POOL_GROUP:(gi + 1) * POOL_GROUP]
        lagged = jnp.pad(c, ((0, 0), (w, 0), (0, 0)))[:, :S]
        means.append((c - lagged) / jnp.minimum(count, float(w)))
    pooled = jnp.concatenate(means, axis=-1).astype(a.dtype) - a
    pooled = pooled.reshape(B, S, len(POOL_WINDOWS), POOL_GROUP)
    mixed = jnp.einsum('bsgc,gcd->bsgd', pooled, pool_w).reshape(B, S, POOL_WIDTH)
    return mixed * pool_scale


def compress_blocks(kv, pos_emb, w1, w2):
    S = kv.shape[1]
    n_cmp = (S - CMP_BLOCK) // CMP_STRIDE + 1
    idx = np.arange(n_cmp)[:, None] * CMP_STRIDE + np.arange(CMP_BLOCK)[None, :]
    blocks = kv[:, idx] + pos_emb[:, None, :]
    h = jax.nn.gelu(jnp.einsum('bnlgd,ldh->bngh', blocks, w1))
    return jnp.einsum('bngh,hd->bngd', h, w2)


def cmp_sel_overlap(n_cmp, n_sel):
    cs = np.arange(n_cmp)[:, None] * CMP_STRIDE
    ss = np.arange(n_sel)[None, :] * SEL_BLOCK
    ov = np.minimum(cs + CMP_BLOCK, ss + SEL_BLOCK) - np.maximum(cs, ss)
    return (np.maximum(ov, 0) / CMP_BLOCK).astype(np.float32)


def nsa_mixer(q, k_cmp, v_cmp, k_slc, v_slc, k_win, v_win, gate_logits,
              cmp_pos_k, cmp_w1_k, cmp_w2_k, cmp_pos_v, cmp_w1_v, cmp_w2_v):
    B, S = q.shape[:2]
    G, HPG, dk = NSA_KV_HEADS, NSA_HPG, NSA_HEAD_DIM
    dt = q.dtype
    scale = dk ** -0.5
    qg = q.reshape(B, S, G, HPG, dk)
    pos = jnp.arange(S)

    n_cmp = (S - CMP_BLOCK) // CMP_STRIDE + 1
    kc = compress_blocks(k_cmp, cmp_pos_k, cmp_w1_k, cmp_w2_k)
    vc = compress_blocks(v_cmp, cmp_pos_v, cmp_w1_v, cmp_w2_v)
    s_cmp = jnp.einsum('bsghd,bngd->bghsn', qg, kc) * scale
    cmp_end = jnp.arange(n_cmp) * CMP_STRIDE + CMP_BLOCK - 1
    p_cmp = masked_softmax(s_cmp, cmp_end[None, :] <= pos[:, None])
    o_cmp = jnp.einsum('bghsn,bngd->bsghd', p_cmp.astype(dt), vc)

    n_sel = S // SEL_BLOCK
    top_k = min(SEL_TOPK, n_sel)
    imp = jnp.einsum('bghsn,nj->bgsj', p_cmp, jnp.asarray(cmp_sel_overlap(n_cmp, n_sel)))
    blk = jnp.arange(n_sel)[None, :]
    cur = (pos // SEL_BLOCK)[:, None]
    forced = (blk == 0) | (blk == cur) | (blk == cur - 1)
    valid = blk * SEL_BLOCK <= pos[:, None]
    imp = jnp.where(forced, 1e9, jnp.where(valid, imp, -1.0))
    _, sel_idx = lax.top_k(imp, top_k)

    n_q = S // Q_BLOCK
    ks_blk = k_slc.reshape(B, n_sel, SEL_BLOCK, G, dk).transpose(0, 3, 1, 2, 4)
    vs_blk = v_slc.reshape(B, n_sel, SEL_BLOCK, G, dk).transpose(0, 3, 1, 2, 4)
    bi = jnp.arange(B)[:, None, None, None]
    gi = jnp.arange(G)[None, :, None, None]
    q_blocks = qg.reshape(B, n_q, Q_BLOCK, G, HPG, dk).transpose(1, 0, 2, 3, 4, 5)
    idx_blocks = sel_idx.reshape(B, G, n_q, Q_BLOCK, top_k).transpose(2, 0, 1, 3, 4)
    starts = jnp.arange(n_q) * Q_BLOCK

    def sel_block(args):
        qb, ib, st = args
        k_sel = ks_blk[bi, gi, ib]
        v_sel = vs_blk[bi, gi, ib]
        s = jnp.einsum('btghd,bgtkld->bghtkl', qb, k_sel) * scale
        t = st + jnp.arange(Q_BLOCK)
        key_pos = ib[..., None] * SEL_BLOCK + jnp.arange(SEL_BLOCK)
        m = (key_pos <= t[None, None, :, None, None]).reshape(B, G, 1, Q_BLOCK, top_k * SEL_BLOCK)
        sh = s.shape
        p = masked_softmax(s.reshape(sh[0], sh[1], sh[2], sh[3], top_k * SEL_BLOCK), m).reshape(sh)
        return jnp.einsum('bghtkl,bgtkld->btghd', p.astype(v_sel.dtype), v_sel)

    o_sel = lax.map(sel_block, (q_blocks, idx_blocks, starts))
    o_sel = o_sel.transpose(1, 0, 2, 3, 4, 5).reshape(B, S, G, HPG, dk)

    n_shift = WINDOW // Q_BLOCK

    def windows(kv):
        kpad = jnp.pad(kv, ((0, 0), (WINDOW, 0), (0, 0), (0, 0)))
        kb = kpad.reshape(B, n_q + n_shift, Q_BLOCK, G, dk)
        return jnp.concatenate([kb[:, i:i + n_q] for i in range(n_shift + 1)], axis=2)

    kwb = windows(k_win)
    vwb = windows(v_win)
    qb_all = qg.reshape(B, n_q, Q_BLOCK, G, HPG, dk)
    s_win = jnp.einsum('bqtghd,bqjgd->bqghtj', qb_all, kwb) * scale
    i = jnp.arange(Q_BLOCK)[:, None]
    j = jnp.arange(WINDOW + Q_BLOCK)[None, :]
    key_pos = starts[:, None, None] - WINDOW + j[None]
    wmask = (j > i) & (j <= i + WINDOW) & (key_pos >= 0)
    p_win = masked_softmax(s_win, wmask[None, :, None, None])
    o_win = jnp.einsum('bqghtj,bqjgd->bqtghd', p_win.astype(dt), vwb).reshape(B, S, G, HPG, dk)

    g = jax.nn.sigmoid(gate_logits.reshape(B, S, G, HPG, 3))
    o = g[..., 0:1] * o_cmp + g[..., 1:2] * o_sel + g[..., 2:3] * o_win
    return o.reshape(B, S, NSA_WIDTH)


def gmlp_mixer(z, ln_g, ln_b, ws, bs):
    B, S, _ = z.shape
    z = jax.nn.gelu(z)
    u, v = jnp.split(z, 2, axis=-1)
    v = layernorm(v, ln_g, ln_b)
    n_chunk = S // GMLP_CHUNK
    v = v.reshape(B, n_chunk, GMLP_CHUNK, GMLP_GROUPS, GMLP_GROUP_DIM)
    causal = jnp.tril(jnp.ones((GMLP_CHUNK, GMLP_CHUNK), dtype=bool))
    w = jnp.where(causal[None], ws, 0.0).astype(v.dtype)
    s = jnp.einsum('gts,bcsgd->bctgd', w, v) + bs.T[None, None, :, :, None]
    return u * s.reshape(B, S, GMLP_WIDTH)


def token_mixer(h, w_in, b_in, pool_w, pool_scale,
                cmp_pos_k, cmp_w1_k, cmp_w2_k, cmp_pos_v, cmp_w1_v, cmp_w2_v,
                gmlp_ln_g, gmlp_ln_b, gmlp_ws, gmlp_bs,
                w_br_pool, w_br_nsa, w_br_gmlp, w_mix_out):
    B, S, _ = h.shape
    z = h @ w_in + b_in
    offs = np.cumsum(IN_SIZES)[:-1].tolist()
    a, q, kv, nsa_g, gm, mg = jnp.split(z, offs, axis=-1)
    q = q.reshape(B, S, NSA_HEADS, NSA_HEAD_DIM)
    kv = kv.reshape(B, S, 6, NSA_KV_HEADS, NSA_HEAD_DIM)
    y_pool = pool_mixer(a, pool_w, pool_scale) @ w_br_pool
    y_nsa = nsa_mixer(q, kv[:, :, 0], kv[:, :, 1], kv[:, :, 2], kv[:, :, 3], kv[:, :, 4], kv[:, :, 5], nsa_g,
                      cmp_pos_k, cmp_w1_k, cmp_w2_k, cmp_pos_v, cmp_w1_v, cmp_w2_v) @ w_br_nsa
    y_gmlp = gmlp_mixer(gm, gmlp_ln_g, gmlp_ln_b, gmlp_ws, gmlp_bs) @ w_br_gmlp
    g_pool, g_nsa, g_gmlp = jnp.split(jax.nn.sigmoid(mg), 3, axis=-1)
    return (g_pool * y_pool + g_nsa * y_nsa + g_gmlp * y_gmlp) @ w_mix_out


def memory_cross_attention(h, mem_n, wq, wk, wv, wo):
    B, S, _ = h.shape
    M = mem_n.shape[1]
    q = (h @ wq).reshape(B, S, MEM_HEADS, MEM_HEAD_DIM)
    k = (mem_n @ wk).reshape(B, M, MEM_HEADS, MEM_HEAD_DIM)
    v = (mem_n @ wv).reshape(B, M, MEM_HEADS, MEM_HEAD_DIM)
    s = jnp.einsum('bshd,bmhd->bhsm', q, k).astype(jnp.float32) * (MEM_HEAD_DIM ** -0.5)
    p = jax.nn.softmax(s, axis=-1).astype(v.dtype)
    o = jnp.einsum('bhsm,bmhd->bshd', p, v).reshape(B, S, D_MODEL)
    return o @ wo


def setup_inputs(seed: int = 0) -> dict:
    key = jax.random.key(seed)
    keys = iter(jax.random.split(key, 48))
    L = DEPTH

    def w(shape, fan_in):
        return jax.random.normal(next(keys), shape, jnp.float32) * fan_in ** -0.5

    def gain(shape):
        return 1.0 + 0.05 * jax.random.normal(next(keys), shape, jnp.float32)

    def small(shape):
        return 0.02 * jax.random.normal(next(keys), shape, jnp.float32)

    return {
        "x": jax.random.normal(next(keys), (BATCH, SEQ, D_MODEL), jnp.float32),
        "mem": jax.random.normal(next(keys), (BATCH, MEM_LEN, D_MODEL), jnp.float32),
        "ff1_pre_g": gain((L, D_MODEL)),
        "ff1_w1": w((L, D_MODEL, D_FF), D_MODEL),
        "ff1_w3": w((L, D_MODEL, D_FF), D_MODEL),
        "ff1_w2": w((L, D_FF, D_MODEL), D_FF),
        "ff1_post_g": gain((L, D_MODEL)),
        "mix_pre_g": gain((L, D_MODEL)),
        "w_in": w((L, D_MODEL, IN_COLS), D_MODEL),
        "b_in": small((L, IN_COLS)),
        "pool_w": w((L, len(POOL_WINDOWS), POOL_GROUP, POOL_GROUP), POOL_GROUP),
        "pool_scale": gain((L, POOL_WIDTH)),
        "cmp_pos_k": small((L, CMP_BLOCK, NSA_HEAD_DIM)),
        "cmp_w1_k": w((L, CMP_BLOCK, NSA_HEAD_DIM, CMP_HIDDEN), CMP_BLOCK * NSA_HEAD_DIM),
        "cmp_w2_k": w((L, CMP_HIDDEN, NSA_HEAD_DIM), CMP_HIDDEN),
        "cmp_pos_v": small((L, CMP_BLOCK, NSA_HEAD_DIM)),
        "cmp_w1_v": w((L, CMP_BLOCK, NSA_HEAD_DIM, CMP_HIDDEN), CMP_BLOCK * NSA_HEAD_DIM),
        "cmp_w2_v": w((L, CMP_HIDDEN, NSA_HEAD_DIM), CMP_HIDDEN),
        "gmlp_ln_g": gain((L, GMLP_WIDTH)),
        "gmlp_ln_b": small((L, GMLP_WIDTH)),
        "gmlp_ws": w((L, GMLP_GROUPS, GMLP_CHUNK, GMLP_CHUNK), GMLP_CHUNK),
        "gmlp_bs": gain((L, GMLP_GROUPS, GMLP_CHUNK)),
        "w_br_pool": w((L, POOL_WIDTH, D_MODEL), POOL_WIDTH),
        "w_br_nsa": w((L, NSA_WIDTH, D_MODEL), NSA_WIDTH),
        "w_br_gmlp": w((L, GMLP_WIDTH, D_MODEL), GMLP_WIDTH),
        "w_mix_out": w((L, D_MODEL, D_MODEL), D_MODEL),
        "mix_post_g": gain((L, D_MODEL)),
        "mem_pre_g": gain((L, D_MODEL)),
        "mem_kv_g": gain((L, D_MODEL)),
        "mem_wq": w((L, D_MODEL, D_MODEL), D_MODEL),
        "mem_wk": w((L, D_MODEL, D_MODEL), D_MODEL),
        "mem_wv": w((L, D_MODEL, D_MODEL), D_MODEL),
        "mem_wo": w((L, D_MODEL, D_MODEL), D_MODEL),
        "mem_post_g": gain((L, D_MODEL)),
        "ff2_pre_g": gain((L, D_MODEL)),
        "ff2_w1": w((L, D_MODEL, D_FF), D_MODEL),
        "ff2_w3": w((L, D_MODEL, D_FF), D_MODEL),
        "ff2_w2": w((L, D_FF, D_MODEL), D_FF),
        "ff2_post_g": gain((L, D_MODEL)),
    }


def reference(x, mem,
              ff1_pre_g, ff1_w1, ff1_w3, ff1_w2, ff1_post_g,
              mix_pre_g, w_in, b_in, pool_w, pool_scale,
              cmp_pos_k, cmp_w1_k, cmp_w2_k, cmp_pos_v, cmp_w1_v, cmp_w2_v,
              gmlp_ln_g, gmlp_ln_b, gmlp_ws, gmlp_bs,
              w_br_pool, w_br_nsa, w_br_gmlp, w_mix_out, mix_post_g,
              mem_pre_g, mem_kv_g, mem_wq, mem_wk, mem_wv, mem_wo, mem_post_g,
              ff2_pre_g, ff2_w1, ff2_w3, ff2_w2, ff2_post_g):
    for l in range(DEPTH):
        h = rmsnorm(x, ff1_pre_g[l])
        x = x + 0.5 * rmsnorm(swiglu(h, ff1_w1[l], ff1_w3[l], ff1_w2[l]), ff1_post_g[l])
        h = rmsnorm(x, mix_pre_g[l])
        y = token_mixer(h, w_in[l], b_in[l], pool_w[l], pool_scale[l],
                        cmp_pos_k[l], cmp_w1_k[l], cmp_w2_k[l], cmp_pos_v[l], cmp_w1_v[l], cmp_w2_v[l],
                        gmlp_ln_g[l], gmlp_ln_b[l], gmlp_ws[l], gmlp_bs[l],
                        w_br_pool[l], w_br_nsa[l], w_br_gmlp[l], w_mix_out[l])
        x = x + rmsnorm(y, mix_post_g[l])
        h = rmsnorm(x, mem_pre_g[l])
        y = memory_cross_attention(h, rmsnorm(mem, mem_kv_g[l]), mem_wq[l], mem_wk[l], mem_wv[l], mem_wo[l])
        x = x + rmsnorm(y, mem_post_g[l])
        h = rmsnorm(x, ff2_pre_g[l])
        x = x + 0.5 * rmsnorm(swiglu(h, ff2_w1[l], ff2_w3[l], ff2_w2[l]), ff2_post_g[l])
    return x
```

```python
import functools

import numpy as np
import jax
import jax.numpy as jnp
from jax import lax
from jax.experimental import pallas as pl
from jax.experimental.pallas import tpu as pltpu

F32 = jnp.float32
BF16 = jnp.bfloat16

EPS = 1e-6
NEG = -1e30

MEM_HEADS = 4
POOL_WINDOWS = (2, 4, 8, 16)
POOL_GROUP = 128
NSA_HEADS = 8
NSA_KV_HEADS = 2
NSA_HPG = NSA_HEADS // NSA_KV_HEADS
NSA_HEAD_DIM = 64
CMP_BLOCK = 32
CMP_STRIDE = 16
SEL_BLOCK = 64
SEL_TOPK = 8
WINDOW = 256
GMLP_WIDTH = 512
GMLP_GROUPS = 4
GMLP_CHUNK = 128
POOL_WIDTH = 512
NSA_WIDTH = NSA_HEADS * NSA_HEAD_DIM

LANES = 128
VMEM_LIMIT_BYTES = 56 * 1024 * 1024

FFN_TM = 512
FFN_CHUNK = 512
PROJ_TM = 256
MERGE_TM = 256
MEM_TM = 256
NSA_TQ = 128


def _dot(a, b):
    return jnp.dot(a, b, preferred_element_type=F32)


def _dot_nt(a, b):
    return lax.dot_general(a, b, (((1,), (1,)), ((), ())), preferred_element_type=F32)


def _rms(x, g):
    return x * lax.rsqrt(jnp.mean(x * x, axis=-1, keepdims=True) + EPS) * g


def _gelu(x):
    c = np.float32(np.sqrt(2.0 / np.pi))
    return x * (0.5 * (1.0 + jnp.tanh(c * (x + 0.044715 * (x * x * x)))))


def _const_spec(shape):
    n = len(shape)
    return pl.BlockSpec(shape, lambda *_: (0,) * n, pipeline_mode=pl.Buffered(1))


def _params(*sem):
    return pltpu.CompilerParams(dimension_semantics=sem, vmem_limit_bytes=VMEM_LIMIT_BYTES)


def _ffn_kernel(x_ref, gpre_ref, w1_ref, w3_ref, w2_ref, gpost_ref, o_ref, h_ref, acc_ref, *, chunks):
    x = x_ref[...]
    h_ref[...] = _rms(x, gpre_ref[...]).astype(BF16)
    for n, (off, width) in enumerate(chunks):
        h = h_ref[...]
        a = _dot(h, w1_ref[:, off:off + width])
        b = _dot(h, w3_ref[:, off:off + width])
        u = (a * jax.nn.sigmoid(a) * b).astype(BF16)
        y = _dot(u, w2_ref[off:off + width, :])
        if n == 0:
            acc_ref[...] = y
        else:
            acc_ref[...] += y
    o_ref[...] = x + 0.5 * _rms(acc_ref[...], gpost_ref[...])


def _ffn(x, gpre, w1, w3, w2, gpost):
    t, d = x.shape
    f = w1.shape[1]
    tm = FFN_TM
    chunks = tuple((off, min(FFN_CHUNK, f - off)) for off in range(0, f, FFN_CHUNK))
    row = pl.BlockSpec((tm, d), lambda i: (i, 0))
    return pl.pallas_call(
        functools.partial(_ffn_kernel, chunks=chunks),
        out_shape=jax.ShapeDtypeStruct((t, d), F32),
        grid=(t // tm,),
        in_specs=[row, _const_spec((1, d)), _const_spec((d, f)), _const_spec((d, f)), _const_spec((f, d)),
                  _const_spec((1, d))],
        out_specs=row,
        scratch_shapes=[pltpu.VMEM((tm, d), BF16), pltpu.VMEM((tm, d), F32)],
        compiler_params=_params("parallel"),
        name="ffn",
    )(x, gpre, w1, w3, w2, gpost)


_A0, _Q0, _KV0, _G0, _GM0, _PROJ_COLS = 0, 512, 1024, 1792, 2048, 3072


def _inproj_kernel(x_ref, g_ref, w_ref, b_ref, ws_ref, lng_ref, lnb_ref, bsx_ref,
                   a_ref, q_ref, kvc_ref, kv4_ref, gate_ref, gm_ref, *, tm):
    h = _rms(x_ref[...], g_ref[...]).astype(BF16)
    z = _dot(h, w_ref[...]) + b_ref[...]
    a_ref[...] = z[:, _A0:_A0 + POOL_WIDTH]
    zq = z[:, _Q0:_Q0 + NSA_WIDTH] * (NSA_HEAD_DIM ** -0.5)
    for hh in range(NSA_HEADS):
        q_ref[hh] = zq[:, hh * 64:(hh + 1) * 64].astype(BF16)
    kvc_ref[...] = z[:, _KV0:_KV0 + 256]
    for r in range(8):
        c0 = _KV0 + 256 + r * 64
        kv4_ref[r] = z[:, c0:c0 + 64].astype(BF16)
    gate_ref[...] = jax.nn.sigmoid(z[:, _G0:_G0 + 256])
    gm = _gelu(z[:, _GM0:_GM0 + 2 * GMLP_WIDTH])
    u = gm[:, :GMLP_WIDTH]
    v = gm[:, GMLP_WIDTH:]
    mu = jnp.mean(v, axis=-1, keepdims=True)
    vc = v - mu
    var = jnp.mean(vc * vc, axis=-1, keepdims=True)
    vn = (vc * lax.rsqrt(var + EPS) * lng_ref[...] + lnb_ref[...]).astype(BF16)
    ri = lax.broadcasted_iota(jnp.int32, (GMLP_CHUNK, GMLP_CHUNK), 0)
    ci = lax.broadcasted_iota(jnp.int32, (GMLP_CHUNK, GMLP_CHUNK), 1)
    for g in range(GMLP_GROUPS):
        wg = jnp.where(ri >= ci, ws_ref[g], 0.0).astype(BF16)
        cs = slice(g * 128, (g + 1) * 128)
        for c in range(tm // GMLP_CHUNK):
            rs = slice(c * GMLP_CHUNK, (c + 1) * GMLP_CHUNK)
            s = _dot(wg, vn[rs, cs]) + bsx_ref[:, cs]
            gm_ref[rs, cs] = (u[rs, cs] * s).astype(BF16)


def _inproj(x, g, w, b, ws, lng, lnb, bsx):
    t, d = x.shape
    tm = PROJ_TM
    row = lambda width: pl.BlockSpec((tm, width), lambda i: (i, 0))
    heads = pl.BlockSpec((8, tm, 64), lambda i: (0, i, 0))
    return pl.pallas_call(
        functools.partial(_inproj_kernel, tm=tm),
        out_shape=(
            jax.ShapeDtypeStruct((t, POOL_WIDTH), F32),
            jax.ShapeDtypeStruct((8, t, 64), BF16),
            jax.ShapeDtypeStruct((t, 256), F32),
            jax.ShapeDtypeStruct((8, t, 64), BF16),
            jax.ShapeDtypeStruct((t, 256), F32),
            jax.ShapeDtypeStruct((t, GMLP_WIDTH), BF16),
        ),
        grid=(t // tm,),
        in_specs=[row(d), _const_spec((1, d)), _const_spec((d, _PROJ_COLS)), _const_spec((1, _PROJ_COLS)),
                  _const_spec((GMLP_GROUPS, GMLP_CHUNK, GMLP_CHUNK)), _const_spec((1, GMLP_WIDTH)),
                  _const_spec((1, GMLP_WIDTH)), _const_spec((GMLP_CHUNK, GMLP_WIDTH))],
        out_specs=(row(POOL_WIDTH), heads, row(256), heads, row(256), row(GMLP_WIDTH)),
        compiler_params=_params("parallel"),
        name="inproj",
    )(x, g, w, b, ws, lng, lnb, bsx)


_POOL_PAD = 16


def _pool_kernel(a_ref, pw_ref, ps_ref, o_ref, pad_ref, *, seq):
    pos = lax.broadcasted_iota(jnp.int32, (seq, 1), 0).astype(F32) + 1.0
    pad_ref[0:_POOL_PAD, :] = jnp.zeros((_POOL_PAD, POOL_GROUP), F32)
    for gi, w in enumerate(POOL_WINDOWS):
        cs = slice(gi * POOL_GROUP, (gi + 1) * POOL_GROUP)
        x = a_ref[:, cs]
        s = x
        k = 1
        while k < w:
            pad_ref[_POOL_PAD:_POOL_PAD + seq, :] = s
            s = s + pad_ref[_POOL_PAD - k:_POOL_PAD - k + seq, :]
            k *= 2
        pooled = (s / jnp.minimum(pos, float(w)) - x).astype(BF16)
        o_ref[:, cs] = (_dot(pooled, pw_ref[gi]) * ps_ref[:, cs]).astype(BF16)


def _pool(a, pw, ps, batch, seq):
    t = a.shape[0]
    blk = pl.BlockSpec((seq, POOL_WIDTH), lambda b: (b, 0))
    return pl.pallas_call(
        functools.partial(_pool_kernel, seq=seq),
        out_shape=jax.ShapeDtypeStruct((t, POOL_WIDTH), BF16),
        grid=(batch,),
        in_specs=[blk, _const_spec((len(POOL_WINDOWS), POOL_GROUP, POOL_GROUP)), _const_spec((1, POOL_WIDTH))],
        out_specs=blk,
        scratch_shapes=[pltpu.VMEM((_POOL_PAD + seq, POOL_GROUP), F32)],
        compiler_params=_params("parallel"),
        name="pool",
    )(a, pw, ps)


def _cmpkv_kernel(a_ref, pos_ref, w1_ref, w2_ref, o_ref):
    a = a_ref[...]
    xa = (a + pos_ref[0:1, :]).astype(BF16)
    xb = (a + pos_ref[1:2, :]).astype(BF16)
    p = _dot(xa, w1_ref[0])
    q = _dot(xb, w1_ref[1])
    nrow = q.shape[0]
    hpre = p + pltpu.roll(q, nrow - 1, axis=0)
    o_ref[...] = _dot(_gelu(hpre).astype(BF16), w2_ref[...]).astype(BF16)


def _cmpkv(a4, pos, w1, w2):
    batch, _, nrow, width = a4.shape
    hid = w1.shape[-1]
    return pl.pallas_call(
        _cmpkv_kernel,
        out_shape=jax.ShapeDtypeStruct((batch, 4, nrow, NSA_HEAD_DIM), BF16),
        grid=(batch, 4),
        in_specs=[pl.BlockSpec((None, None, nrow, width), lambda b, j: (b, j, 0, 0)),
                  pl.BlockSpec((None, 2, width), lambda b, j: (j // 2, 0, 0)),
                  pl.BlockSpec((None, 2, width, hid), lambda b, j: (j // 2, 0, 0, 0)),
                  pl.BlockSpec((None, hid, NSA_HEAD_DIM), lambda b, j: (j // 2, 0, 0))],
        out_specs=pl.BlockSpec((None, None, nrow, NSA_HEAD_DIM), lambda b, j: (b, j, 0, 0)),
        compiler_params=_params("parallel", "arbitrary"),
        name="cmpkv",
    )(a4, pos, w1, w2)


def _nsa_kernel(q_ref, kc_ref, vc_ref, ks_ref, vs_ref, kw_ref, vw_ref, gate_ref, ov_ref, ex_ref, o_ref, *, tq):
    i = pl.program_id(2)
    m4 = NSA_HPG * tq
    q4 = q_ref[...].reshape(m4, NSA_HEAD_DIM)
    rowi = lax.broadcasted_iota(jnp.int32, (m4, 1), 0)
    t = i * tq + jnp.bitwise_and(rowi, tq - 1)
    lane = lax.broadcasted_iota(jnp.int32, (1, LANES), 1)

    s = _dot_nt(q4, kc_ref[...])
    valid = (lane * CMP_STRIDE + (CMP_BLOCK - 1)) <= t
    s = jnp.where(valid, s, NEG)
    e = jnp.exp(s - jnp.max(s, axis=-1, keepdims=True))
    p = jnp.where(valid, e / jnp.sum(e, axis=-1, keepdims=True), 0.0)
    o_cmp = _dot(p.astype(BF16), vc_ref[...])

    psum = p[0:tq] + p[tq:2 * tq] + p[2 * tq:3 * tq] + p[3 * tq:4 * tq]
    p_hi = psum.astype(BF16)
    p_lo = (psum - p_hi.astype(F32)).astype(BF16)
    imp = _dot(p_hi, ov_ref[...]) + _dot(p_lo, ov_ref[...])
    tqpos = i * tq + lax.broadcasted_iota(jnp.int32, (tq, 1), 0)
    cur = jnp.right_shift(tqpos, 6)
    forced = (lane == 0) | (lane == cur) | (lane == cur - 1)
    val = jnp.where(forced, 1e9, jnp.where(lane * SEL_BLOCK <= tqpos, imp, -1.0))
    n_sel = ex_ref.shape[1] // SEL_BLOCK
    val = jnp.where(lane < n_sel, val, -2.0)
    vt = val.T[0:n_sel, :]
    jidx = lax.broadcasted_iota(jnp.int32, (n_sel, 1), 0)
    cnt = jnp.zeros((n_sel, tq), F32)
    for jp in range(n_sel):
        r = vt[jp:jp + 1, :]
        tie = jnp.where(jidx > jp, 1.0, 0.0)
        cnt = cnt + jnp.where(r > vt, 1.0, jnp.where(r == vt, tie, 0.0))
    sel_t = jnp.where(cnt < float(SEL_TOPK), 1.0, 0.0)
    sel = jnp.concatenate([sel_t, jnp.zeros((LANES - n_sel, tq), F32)], axis=0).T.astype(BF16)

    def slc_body(kt, carry):
        m_run, l_run, acc = carry
        off = pl.multiple_of(kt * tq, tq)
        s = _dot_nt(q4, ks_ref[pl.ds(off, tq), :])
        chosen = _dot(sel, ex_ref[:, pl.ds(off, tq)])
        chosen4 = jnp.concatenate([chosen] * NSA_HPG, axis=0)
        mask = (chosen4 > 0.5) & ((off + lane) <= t)
        s = jnp.where(mask, s, NEG)
        m_new = jnp.maximum(m_run, jnp.max(s, axis=-1, keepdims=True))
        alpha = jnp.exp(m_run - m_new)
        e = jnp.where(mask, jnp.exp(s - m_new), 0.0)
        l_new = alpha * l_run + jnp.sum(e, axis=-1, keepdims=True)
        acc = alpha * acc + _dot(e.astype(BF16), vs_ref[pl.ds(off, tq), :])
        return m_new, l_new, acc

    m0 = jnp.full((m4, 1), NEG, F32)
    l0 = jnp.zeros((m4, 1), F32)
    acc0 = jnp.zeros((m4, NSA_HEAD_DIM), F32)
    _, l_slc, acc_slc = lax.fori_loop(0, i + 1, slc_body, (m0, l0, acc0))
    o_slc = acc_slc / l_slc

    s_parts, offs = [], []
    for d in range(WINDOW // tq + 1):
        kt = i - WINDOW // tq + d
        off = pl.multiple_of(jnp.maximum(kt, 0) * tq, tq)
        kp = kt * tq + lane
        mask = (kp > t - WINDOW) & (kp <= t) & (kp >= 0)
        s_parts.append(jnp.where(mask, _dot_nt(q4, kw_ref[pl.ds(off, tq), :]), NEG))
        offs.append(off)
    m_w = functools.reduce(jnp.maximum, [jnp.max(sp, axis=-1, keepdims=True) for sp in s_parts])
    e_parts = [jnp.exp(sp - m_w) for sp in s_parts]
    l_w = functools.reduce(jnp.add, [jnp.sum(ep, axis=-1, keepdims=True) for ep in e_parts])
    acc_w = functools.reduce(
        jnp.add, [_dot(ep.astype(BF16), vw_ref[pl.ds(off, tq), :]) for ep, off in zip(e_parts, offs)])
    o_win = acc_w / l_w

    gt = gate_ref[...]
    for h in range(NSA_HPG):
        rs = slice(h * tq, (h + 1) * tq)
        o = (gt[:, 3 * h:3 * h + 1] * o_cmp[rs] + gt[:, 3 * h + 1:3 * h + 2] * o_slc[rs]
             + gt[:, 3 * h + 2:3 * h + 3] * o_win[rs])
        o_ref[h] = o.astype(BF16)


def _nsa(q_hm, cmp, kv4, gates, ov, ex, batch, seq):
    t = q_hm.shape[1]
    tq = NSA_TQ
    nq = seq // tq
    ncmp = cmp.shape[2]
    qblk = pl.BlockSpec((NSA_HPG, tq, NSA_HEAD_DIM), lambda b, g, i: (g, b * nq + i, 0))
    cblk = lambda base: pl.BlockSpec((None, None, ncmp, NSA_HEAD_DIM), lambda b, g, i: (b, base + g, 0, 0))
    kvblk = lambda base: pl.BlockSpec((None, seq, NSA_HEAD_DIM), lambda b, g, i: (base + g, b, 0))
    return pl.pallas_call(
        functools.partial(_nsa_kernel, tq=tq),
        out_shape=jax.ShapeDtypeStruct((NSA_HEADS, t, NSA_HEAD_DIM), BF16),
        grid=(batch, NSA_KV_HEADS, nq),
        in_specs=[qblk, cblk(0), cblk(2), kvblk(0), kvblk(2), kvblk(4), kvblk(6),
                  pl.BlockSpec((tq, LANES), lambda b, g, i: (b * nq + i, g)),
                  _const_spec(ov.shape), _const_spec(ex.shape)],
        out_specs=qblk,
        compiler_params=_params("parallel", "parallel", "arbitrary"),
        name="nsa",
    )(q_hm, cmp, cmp, kv4, kv4, kv4, kv4, gates, ov, ex)


def _merge_kernel(x_ref, pm_ref, on_ref, gm_ref, gpre_ref, wmg_ref, bmg_ref, wbp_ref, wbn_ref, wbg_ref,
                  wout_ref, gpost_ref, o_ref):
    x = x_ref[...]
    d = x.shape[1]
    h = _rms(x, gpre_ref[...]).astype(BF16)

    def gate(j):
        return jax.nn.sigmoid(_dot(h, wmg_ref[:, j * d:(j + 1) * d]) + bmg_ref[:, j * d:(j + 1) * d])

    y_nsa = _dot(on_ref[0], wbn_ref[0])
    for hh in range(1, NSA_HEADS):
        y_nsa = y_nsa + _dot(on_ref[hh], wbn_ref[hh])
    comb = gate(0) * _dot(pm_ref[...], wbp_ref[...])
    comb = comb + gate(1) * y_nsa
    comb = comb + gate(2) * _dot(gm_ref[...], wbg_ref[...])
    y = _dot(comb.astype(BF16), wout_ref[...])
    o_ref[...] = x + _rms(y, gpost_ref[...])


def _merge(x, pm, o_hm, gm, gpre, wmg, bmg, wbp, wbn, wbg, wout, gpost):
    t, d = x.shape
    tm = MERGE_TM
    row = lambda width: pl.BlockSpec((tm, width), lambda i: (i, 0))
    return pl.pallas_call(
        _merge_kernel,
        out_shape=jax.ShapeDtypeStruct((t, d), F32),
        grid=(t // tm,),
        in_specs=[row(d), row(POOL_WIDTH), pl.BlockSpec((NSA_HEADS, tm, NSA_HEAD_DIM), lambda i: (0, i, 0)),
                  row(GMLP_WIDTH), _const_spec((1, d)), _const_spec((d, 3 * d)), _const_spec((1, 3 * d)),
                  _const_spec((POOL_WIDTH, d)), _const_spec((NSA_HEADS, NSA_HEAD_DIM, d)),
                  _const_spec((GMLP_WIDTH, d)), _const_spec((d, d)), _const_spec((1, d))],
        out_specs=row(d),
        compiler_params=_params("parallel"),
        name="merge",
    )(x, pm, o_hm, gm, gpre, wmg, bmg, wbp, wbn, wbg, wout, gpost)


def _memkv_kernel(m_ref, g_ref, wk_ref, wv_ref, k_ref, v_ref):
    mn = _rms(m_ref[...], g_ref[...]).astype(BF16)
    k_ref[...] = _dot(mn, wk_ref[...]).astype(BF16)
    v_ref[...] = _dot(mn, wv_ref[...]).astype(BF16)


def _memkv(mem, g, wk, wv):
    batch, mlen, d = mem.shape
    blk = pl.BlockSpec((None, mlen, d), lambda b: (b, 0, 0))
    out = jax.ShapeDtypeStruct((batch, mlen, d), BF16)
    return pl.pallas_call(
        _memkv_kernel,
        out_shape=(out, out),
        grid=(batch,),
        in_specs=[blk, _const_spec((1, d)), _const_spec((d, d)), _const_spec((d, d))],
        out_specs=(blk, blk),
        compiler_params=_params("parallel"),
        name="memkv",
    )(mem, g, wk, wv)


def _memattn_kernel(x_ref, gpre_ref, wq_ref, k_ref, v_ref, wo_ref, gpost_ref, o_ref):
    x = x_ref[...]
    d = x.shape[1]
    hd = d // MEM_HEADS
    h = _rms(x, gpre_ref[...]).astype(BF16)
    q = (_dot(h, wq_ref[...]) * (hd ** -0.5)).astype(BF16)
    outs = []
    for n in range(MEM_HEADS):
        cs = slice(n * hd, (n + 1) * hd)
        s = _dot_nt(q[:, cs], k_ref[:, cs])
        e = jnp.exp(s - jnp.max(s, axis=-1, keepdims=True))
        p = e / jnp.sum(e, axis=-1, keepdims=True)
        outs.append(_dot(p.astype(BF16), v_ref[:, cs]).astype(BF16))
    y = _dot(jnp.concatenate(outs, axis=1), wo_ref[...])
    o_ref[...] = x + _rms(y, gpost_ref[...])


def _memattn(x, gpre, wq, k, v, wo, gpost, batch, seq):
    t, d = x.shape
    mlen = k.shape[1]
    tm = MEM_TM
    nt = seq // tm
    row = pl.BlockSpec((tm, d), lambda b, i: (b * nt + i, 0))
    kvb = pl.BlockSpec((None, mlen, d), lambda b, i: (b, 0, 0))
    return pl.pallas_call(
        _memattn_kernel,
        out_shape=jax.ShapeDtypeStruct((t, d), F32),
        grid=(batch, nt),
        in_specs=[row, _const_spec((1, d)), _const_spec((d, d)), kvb, kvb, _const_spec((d, d)),
                  _const_spec((1, d))],
        out_specs=row,
        compiler_params=_params("parallel", "arbitrary"),
        name="memattn",
    )(x, gpre, wq, k, v, wo, gpost)


def _overlap_matrix(n_cmp_pad, n_sel):
    cs = np.arange(n_cmp_pad)[:, None] * CMP_STRIDE
    ss = np.arange(LANES)[None, :] * SEL_BLOCK
    ov = np.minimum(cs + CMP_BLOCK, ss + SEL_BLOCK) - np.maximum(cs, ss)
    ov = np.maximum(ov, 0) / CMP_BLOCK
    ov[:, n_sel:] = 0.0
    return jnp.asarray(ov, dtype=BF16)


def _expand_matrix(seq):
    ex = (np.arange(seq)[None, :] // SEL_BLOCK) == np.arange(LANES)[:, None]
    return jnp.asarray(ex, dtype=BF16)


def kernel(x, mem, ff1_pre_g, ff1_w1, ff1_w3, ff1_w2, ff1_post_g, mix_pre_g, w_in, b_in, pool_w, pool_scale, cmp_pos_k, cmp_w1_k, cmp_w2_k, cmp_pos_v, cmp_w1_v, cmp_w2_v, gmlp_ln_g, gmlp_ln_b, gmlp_ws, gmlp_bs, w_br_pool, w_br_nsa, w_br_gmlp, w_mix_out, mix_post_g, mem_pre_g, mem_kv_g, mem_wq, mem_wk, mem_wv, mem_wo, mem_post_g, ff2_pre_g, ff2_w1, ff2_w3, ff2_w2, ff2_post_g):
    batch, seq, d = x.shape
    depth = w_in.shape[0]
    t = batch * seq
    n_cmp_rows = seq // CMP_STRIDE
    bf = lambda w: w.astype(BF16)
    vec = lambda g: g.reshape(depth, 1, -1)

    c_g = 1792
    c_gm = c_g + 3 * NSA_HEADS
    c_mg = c_gm + 2 * GMLP_WIDTH
    zeros_w = lambda n: jnp.zeros((depth, d, n), w_in.dtype)
    zeros_b = lambda n: jnp.zeros((depth, n), b_in.dtype)
    half = 3 * NSA_HPG
    w_proj = bf(jnp.concatenate(
        [w_in[:, :, :c_g], w_in[:, :, c_g:c_g + half], zeros_w(LANES - half),
         w_in[:, :, c_g + half:c_gm], zeros_w(LANES - half), w_in[:, :, c_gm:c_mg]], axis=2))
    b_proj = jnp.concatenate(
        [b_in[:, :c_g], b_in[:, c_g:c_g + half], zeros_b(LANES - half),
         b_in[:, c_g + half:c_gm], zeros_b(LANES - half), b_in[:, c_gm:c_mg]], axis=1).reshape(depth, 1, -1)
    w_mg = bf(w_in[:, :, c_mg:])
    b_mg = b_in[:, c_mg:].reshape(depth, 1, -1)
    bsx = jnp.broadcast_to(jnp.swapaxes(gmlp_bs, 1, 2)[:, :, :, None],
                           (depth, GMLP_CHUNK, GMLP_GROUPS, GMLP_WIDTH // GMLP_GROUPS)).reshape(
                               depth, GMLP_CHUNK, GMLP_WIDTH)
    half_w = CMP_STRIDE * NSA_HEAD_DIM
    cmp_pos = jnp.stack([cmp_pos_k, cmp_pos_v], axis=1).reshape(depth, 2, 2, half_w)
    cmp_w1 = bf(jnp.stack([cmp_w1_k, cmp_w1_v], axis=1)).reshape(depth, 2, 2, half_w, -1)
    cmp_w2 = bf(jnp.stack([cmp_w2_k, cmp_w2_v], axis=1))
    w_br_nsa_h = bf(w_br_nsa).reshape(depth, NSA_HEADS, NSA_HEAD_DIM, d)
    ff1 = (vec(ff1_pre_g), bf(ff1_w1), bf(ff1_w3), bf(ff1_w2), vec(ff1_post_g))
    ff2 = (vec(ff2_pre_g), bf(ff2_w1), bf(ff2_w3), bf(ff2_w2), vec(ff2_post_g))
    pool_w_b, pool_scale_v = bf(pool_w), vec(pool_scale)
    w_br_pool_b, w_br_gmlp_b, w_mix_out_b = bf(w_br_pool), bf(w_br_gmlp), bf(w_mix_out)
    mem_wq_b, mem_wk_b, mem_wv_b, mem_wo_b = bf(mem_wq), bf(mem_wk), bf(mem_wv), bf(mem_wo)
    mix_pre, mix_post = vec(mix_pre_g), vec(mix_post_g)
    mem_pre, mem_kv, mem_post = vec(mem_pre_g), vec(mem_kv_g), vec(mem_post_g)
    ln_g, ln_b = vec(gmlp_ln_g), vec(gmlp_ln_b)
    ov = _overlap_matrix(n_cmp_rows, seq // SEL_BLOCK)
    ex = _expand_matrix(seq)

    xs = x.reshape(t, d)
    for l in range(depth):
        xs = _ffn(xs, *(p[l] for p in ff1))
        a, q_hm, kvc, kv4, gates, gm = _inproj(xs, mix_pre[l], w_proj[l], b_proj[l], gmlp_ws[l], ln_g[l],
                                               ln_b[l], bsx[l])
        pm = _pool(a, pool_w_b[l], pool_scale_v[l], batch, seq)
        a4 = kvc.reshape(batch, n_cmp_rows, CMP_STRIDE, 4, NSA_HEAD_DIM).transpose(0, 3, 1, 2, 4).reshape(
            batch, 4, n_cmp_rows, half_w)
        cmp = _cmpkv(a4, cmp_pos[l], cmp_w1[l], cmp_w2[l])
        o_hm = _nsa(q_hm, cmp, kv4, gates, ov, ex, batch, seq)
        xs = _merge(xs, pm, o_hm, gm, mix_pre[l], w_mg[l], b_mg[l], w_br_pool_b[l], w_br_nsa_h[l],
                    w_br_gmlp_b[l], w_mix_out_b[l], mix_post[l])
        mk, mv = _memkv(mem, mem_kv[l], mem_wk_b[l], mem_wv_b[l])
        xs = _memattn(xs, mem_pre[l], mem_wq_b[l], mk, mv, mem_wo_b[l], mem_post[l], batch, seq)
        xs = _ffn(xs, *(p[l] for p in ff2))
    return xs.reshape(batch, seq, d)
```

```python
import functools

import numpy as np
import jax
import jax.numpy as jnp
from jax import lax
from jax.experimental import pallas as pl
from jax.experimental.pallas import tpu as pltpu

F32 = jnp.float32
BF16 = jnp.bfloat16

EPS = 1e-6
NEG = -1e30
MASK_BIG = 2.0 ** 100
LOG2E = float(np.log2(np.e))

MEM_HEADS = 4
POOL_WINDOWS = (2, 4, 8, 16)
POOL_GROUP = 128
NSA_HEADS = 8
NSA_KV_HEADS = 2
NSA_HPG = NSA_HEADS // NSA_KV_HEADS
NSA_HEAD_DIM = 64
CMP_BLOCK = 32
CMP_STRIDE = 16
SEL_BLOCK = 64
SEL_TOPK = 8
WINDOW = 256
GMLP_WIDTH = 512
GMLP_GROUPS = 4
GMLP_CHUNK = 128
POOL_WIDTH = 512
NSA_WIDTH = NSA_HEADS * NSA_HEAD_DIM

LANES = 128
VMEM_LIMIT_BYTES = 56 * 1024 * 1024

FFN_TM = 512
FFN_CHUNK = 512
PROJ_TM = 256
MERGE_TM = 256
MEM_TM = 256
NSA_TQ = 256
NSA_TK = 256


def _dot(a, b):
    return jnp.dot(a, b, preferred_element_type=F32)


def _dot_nt(a, b):
    return lax.dot_general(a, b, (((1,), (1,)), ((), ())), preferred_element_type=F32)


def _rms(x, g):
    return x * lax.rsqrt(jnp.mean(x * x, axis=-1, keepdims=True) + EPS) * g


def _gelu(x):
    c = np.float32(np.sqrt(2.0 / np.pi))
    return x * (0.5 * (1.0 + jnp.tanh(c * (x + 0.044715 * (x * x * x)))))


def _const_spec(shape):
    n = len(shape)
    return pl.BlockSpec(shape, lambda *_: (0,) * n, pipeline_mode=pl.Buffered(1))


def _params(*sem):
    return pltpu.CompilerParams(dimension_semantics=sem, vmem_limit_bytes=VMEM_LIMIT_BYTES)


def _ffn_kernel(x_ref, gpre_ref, w1_ref, w3_ref, w2_ref, gpost_ref, o_ref, h_ref, acc_ref, *, chunks):
    x = x_ref[...]
    h_ref[...] = _rms(x, gpre_ref[...]).astype(BF16)
    for n, (off, width) in enumerate(chunks):
        h = h_ref[...]
        a = _dot(h, w1_ref[:, off:off + width])
        b = _dot(h, w3_ref[:, off:off + width])
        u = (a * jax.nn.sigmoid(a) * b).astype(BF16)
        y = _dot(u, w2_ref[off:off + width, :])
        if n == 0:
            acc_ref[...] = y
        else:
            acc_ref[...] += y
    o_ref[...] = x + 0.5 * _rms(acc_ref[...], gpost_ref[...])


def _ffn(x, gpre, w1, w3, w2, gpost):
    t, d = x.shape
    f = w1.shape[1]
    tm = FFN_TM
    chunks = tuple((off, min(FFN_CHUNK, f - off)) for off in range(0, f, FFN_CHUNK))
    row = pl.BlockSpec((tm, d), lambda i: (i, 0))
    return pl.pallas_call(
        functools.partial(_ffn_kernel, chunks=chunks),
        out_shape=jax.ShapeDtypeStruct((t, d), F32),
        grid=(t // tm,),
        in_specs=[row, _const_spec((1, d)), _const_spec((d, f)), _const_spec((d, f)), _const_spec((f, d)),
                  _const_spec((1, d))],
        out_specs=row,
        scratch_shapes=[pltpu.VMEM((tm, d), BF16), pltpu.VMEM((tm, d), F32)],
        compiler_params=_params("parallel"),
        name="ffn",
    )(x, gpre, w1, w3, w2, gpost)


_A0, _Q0, _KV0, _G0, _GM0, _PROJ_COLS = 0, 512, 1024, 1792, 2048, 3072


def _inproj_kernel(x_ref, g_ref, w_ref, b_ref, ws_ref, lng_ref, lnb_ref, bsx_ref,
                   a_ref, q_ref, kvc_ref, kv4_ref, gate_ref, gm_ref, *, tm):
    h = _rms(x_ref[...], g_ref[...]).astype(BF16)
    z = _dot(h, w_ref[...]) + b_ref[...]
    a_ref[...] = z[:, _A0:_A0 + POOL_WIDTH]
    zq = z[:, _Q0:_Q0 + NSA_WIDTH] * (NSA_HEAD_DIM ** -0.5 * LOG2E)
    for hh in range(NSA_HEADS):
        q_ref[hh] = zq[:, hh * 64:(hh + 1) * 64].astype(BF16)
    kvc_ref[...] = z[:, _KV0:_KV0 + 256]
    for r in range(8):
        c0 = _KV0 + 256 + r * 64
        kv4_ref[r] = z[:, c0:c0 + 64].astype(BF16)
    gate_ref[...] = jax.nn.sigmoid(z[:, _G0:_G0 + 256])
    gm = _gelu(z[:, _GM0:_GM0 + 2 * GMLP_WIDTH])
    u = gm[:, :GMLP_WIDTH]
    v = gm[:, GMLP_WIDTH:]
    mu = jnp.mean(v, axis=-1, keepdims=True)
    vc = v - mu
    var = jnp.mean(vc * vc, axis=-1, keepdims=True)
    vn = (vc * lax.rsqrt(var + EPS) * lng_ref[...] + lnb_ref[...]).astype(BF16)
    ri = lax.broadcasted_iota(jnp.int32, (GMLP_CHUNK, GMLP_CHUNK), 0)
    ci = lax.broadcasted_iota(jnp.int32, (GMLP_CHUNK, GMLP_CHUNK), 1)
    for g in range(GMLP_GROUPS):
        wg = jnp.where(ri >= ci, ws_ref[g], 0.0).astype(BF16)
        cs = slice(g * 128, (g + 1) * 128)
        for c in range(tm // GMLP_CHUNK):
            rs = slice(c * GMLP_CHUNK, (c + 1) * GMLP_CHUNK)
            s = _dot(wg, vn[rs, cs]) + bsx_ref[:, cs]
            gm_ref[rs, cs] = (u[rs, cs] * s).astype(BF16)


def _inproj(x, g, w, b, ws, lng, lnb, bsx):
    t, d = x.shape
    tm = PROJ_TM
    row = lambda width: pl.BlockSpec((tm, width), lambda i: (i, 0))
    heads = pl.BlockSpec((8, tm, 64), lambda i: (0, i, 0))
    return pl.pallas_call(
        functools.partial(_inproj_kernel, tm=tm),
        out_shape=(
            jax.ShapeDtypeStruct((t, POOL_WIDTH), F32),
            jax.ShapeDtypeStruct((8, t, 64), BF16),
            jax.ShapeDtypeStruct((t, 256), F32),
            jax.ShapeDtypeStruct((8, t, 64), BF16),
            jax.ShapeDtypeStruct((t, 256), F32),
            jax.ShapeDtypeStruct((t, GMLP_WIDTH), BF16),
        ),
        grid=(t // tm,),
        in_specs=[row(d), _const_spec((1, d)), _const_spec((d, _PROJ_COLS)), _const_spec((1, _PROJ_COLS)),
                  _const_spec((GMLP_GROUPS, GMLP_CHUNK, GMLP_CHUNK)), _const_spec((1, GMLP_WIDTH)),
                  _const_spec((1, GMLP_WIDTH)), _const_spec((GMLP_CHUNK, GMLP_WIDTH))],
        out_specs=(row(POOL_WIDTH), heads, row(256), heads, row(256), row(GMLP_WIDTH)),
        compiler_params=_params("parallel"),
        name="inproj",
    )(x, g, w, b, ws, lng, lnb, bsx)


_POOL_PAD = 16


def _pool_kernel(a_ref, pw_ref, ps_ref, o_ref, pad_ref, *, seq):
    pos = lax.broadcasted_iota(jnp.int32, (seq, 1), 0).astype(F32) + 1.0
    pad_ref[0:_POOL_PAD, :] = jnp.zeros((_POOL_PAD, POOL_GROUP), F32)
    for gi, w in enumerate(POOL_WINDOWS):
        cs = slice(gi * POOL_GROUP, (gi + 1) * POOL_GROUP)
        x = a_ref[:, cs]
        s = x
        k = 1
        while k < w:
            pad_ref[_POOL_PAD:_POOL_PAD + seq, :] = s
            s = s + pad_ref[_POOL_PAD - k:_POOL_PAD - k + seq, :]
            k *= 2
        pooled = (s / jnp.minimum(pos, float(w)) - x).astype(BF16)
        o_ref[:, cs] = (_dot(pooled, pw_ref[gi]) * ps_ref[:, cs]).astype(BF16)


def _pool(a, pw, ps, batch, seq):
    t = a.shape[0]
    blk = pl.BlockSpec((seq, POOL_WIDTH), lambda b: (b, 0))
    return pl.pallas_call(
        functools.partial(_pool_kernel, seq=seq),
        out_shape=jax.ShapeDtypeStruct((t, POOL_WIDTH), BF16),
        grid=(batch,),
        in_specs=[blk, _const_spec((len(POOL_WINDOWS), POOL_GROUP, POOL_GROUP)), _const_spec((1, POOL_WIDTH))],
        out_specs=blk,
        scratch_shapes=[pltpu.VMEM((_POOL_PAD + seq, POOL_GROUP), F32)],
        compiler_params=_params("parallel"),
        name="pool",
    )(a, pw, ps)


def _cmpkv_kernel(a_ref, pos_ref, w1_ref, w2_ref, o_ref):
    a = a_ref[...]
    xa = (a + pos_ref[0:1, :]).astype(BF16)
    xb = (a + pos_ref[1:2, :]).astype(BF16)
    p = _dot(xa, w1_ref[0])
    q = _dot(xb, w1_ref[1])
    nrow = q.shape[0]
    hpre = p + pltpu.roll(q, nrow - 1, axis=0)
    o_ref[...] = _dot(_gelu(hpre).astype(BF16), w2_ref[...]).astype(BF16)


def _cmpkv(a4, pos, w1, w2):
    batch, _, nrow, width = a4.shape
    hid = w1.shape[-1]
    return pl.pallas_call(
        _cmpkv_kernel,
        out_shape=jax.ShapeDtypeStruct((batch, 4, nrow, NSA_HEAD_DIM), BF16),
        grid=(batch, 4),
        in_specs=[pl.BlockSpec((None, None, nrow, width), lambda b, j: (b, j, 0, 0)),
                  pl.BlockSpec((None, 2, width), lambda b, j: (j // 2, 0, 0)),
                  pl.BlockSpec((None, 2, width, hid), lambda b, j: (j // 2, 0, 0, 0)),
                  pl.BlockSpec((None, hid, NSA_HEAD_DIM), lambda b, j: (j // 2, 0, 0))],
        out_specs=pl.BlockSpec((None, None, nrow, NSA_HEAD_DIM), lambda b, j: (b, j, 0, 0)),
        compiler_params=_params("parallel", "arbitrary"),
        name="cmpkv",
    )(a4, pos, w1, w2)


def _transpose_bf16(eye, x):
    return _dot_nt(eye, x).astype(BF16)


def _nsa_kernel(q_ref, kc_ref, vc_ref, ks_ref, vs_ref, kw_ref, vw_ref, gate_ref, ovt_ref, extn_ref, o_ref,
                kaug_ref, vst_ref, vwt_ref, *, tq, tk):
    i = pl.program_id(2)
    hd = NSA_HEAD_DIM
    m4 = NSA_HPG * tq
    seq = ks_ref.shape[0]
    n_sel = seq // SEL_BLOCK
    eye = (lax.broadcasted_iota(jnp.int32, (hd, hd), 0)
           == lax.broadcasted_iota(jnp.int32, (hd, hd), 1)).astype(BF16)

    @pl.when(i == 0)
    def _():
        place = (lax.broadcasted_iota(jnp.int32, (hd, LANES), 0)
                 == lax.broadcasted_iota(jnp.int32, (hd, LANES), 1)).astype(BF16)
        kaug_ref[:, 0:LANES] = extn_ref[...]
        for kt in range(seq // tk):
            rs = slice(kt * tk, (kt + 1) * tk)
            kaug_ref[rs, LANES:2 * LANES] = _dot(ks_ref[rs, :], place).astype(BF16)
            vst_ref[:, rs] = _transpose_bf16(eye, vs_ref[rs, :])
            vwt_ref[:, rs] = _transpose_bf16(eye, vw_ref[rs, :])

    qt = jnp.concatenate([_transpose_bf16(eye, q_ref[h]) for h in range(NSA_HPG)], axis=1)
    col = lax.broadcasted_iota(jnp.int32, (1, m4), 1)
    t4 = i * tq + jnp.bitwise_and(col, tq - 1)
    t1 = i * tq + lax.broadcasted_iota(jnp.int32, (1, tq), 1)

    ncmp = kc_ref.shape[0]
    crow = lax.broadcasted_iota(jnp.int32, (ncmp, 1), 0)
    valid = (crow * CMP_STRIDE + (CMP_BLOCK - 1)) <= t4
    s = jnp.where(valid, _dot(kc_ref[...], qt), NEG)
    e = jnp.exp2(s - jnp.max(s, axis=0, keepdims=True))
    p = jnp.where(valid, e / jnp.sum(e, axis=0, keepdims=True), 0.0)
    o_cmp = _dot(_transpose_bf16(eye, vc_ref[...]), p.astype(BF16))

    psum = p[:, 0:tq] + p[:, tq:2 * tq] + p[:, 2 * tq:3 * tq] + p[:, 3 * tq:4 * tq]
    p_hi = psum.astype(BF16)
    p_lo = (psum - p_hi.astype(F32)).astype(BF16)
    imp = (_dot(ovt_ref[...], p_hi) + _dot(ovt_ref[...], p_lo))[0:n_sel, :]
    jrow = lax.broadcasted_iota(jnp.int32, (n_sel, 1), 0)
    cur = jnp.right_shift(t1, 6)
    forced = (jrow == 0) | (jrow == cur) | (jrow == cur - 1)
    vt = jnp.where(forced, 1e9, jnp.where(jrow * SEL_BLOCK <= t1, imp, -1.0))
    n_acc = 4
    cnt = [jnp.zeros((n_sel, tq), F32) for _ in range(n_acc)]
    for jp in range(n_sel):
        r = vt[jp:jp + 1, :]
        tie = jnp.where(jrow > jp, 1.0, 0.0)
        cnt[jp % n_acc] = cnt[jp % n_acc] + jnp.where(r > vt, 1.0, jnp.where(r == vt, tie, 0.0))
    rank = (cnt[0] + cnt[1]) + (cnt[2] + cnt[3])
    notsel = jnp.where(rank < float(SEL_TOPK), 0.0, 1.0)
    notsel = jnp.concatenate([notsel, jnp.zeros((LANES - n_sel, tq), F32)], axis=0).astype(BF16)
    qaug = jnp.concatenate([jnp.concatenate([notsel] * NSA_HPG, axis=1), qt, jnp.zeros((hd, m4), BF16)], axis=0)

    def attend(s, vt_tile, m_run, l_run, acc):
        m_new = jnp.maximum(m_run, jnp.max(s, axis=0, keepdims=True))
        alpha = jnp.exp2(m_run - m_new)
        e = jnp.exp2(s - m_new)
        l_new = alpha * l_run + jnp.sum(e, axis=0, keepdims=True)
        return m_new, l_new, alpha * acc + _dot(vt_tile, e.astype(BF16))

    m0 = jnp.full((1, m4), NEG, F32)
    l0 = jnp.zeros((1, m4), F32)
    acc0 = jnp.zeros((hd, m4), F32)
    krow = lax.broadcasted_iota(jnp.int32, (tk, 1), 0)

    def slc_body(kt, carry):
        off = pl.multiple_of(kt * tk, tk)
        return attend(_dot(kaug_ref[pl.ds(off, tk), :], qaug), vst_ref[:, pl.ds(off, tk)], *carry)

    n_full = i * (tq // tk)
    carry = lax.fori_loop(0, n_full, slc_body, (m0, l0, acc0))
    for d in range(tq // tk):
        off = pl.multiple_of((n_full + d) * tk, tk)
        causal = jnp.where((off + krow) <= t1, 0.0, NEG)
        s = _dot(kaug_ref[pl.ds(off, tk), :], qaug) + jnp.concatenate([causal] * NSA_HPG, axis=1)
        carry = attend(s, vst_ref[:, pl.ds(off, tk)], *carry)
    o_slc = carry[2] / carry[1]

    carry = (m0, l0, acc0)
    n_win = (WINDOW + tq) // tk
    for d in range(n_win):
        kt = (i * tq + tq) // tk - n_win + d
        off = pl.multiple_of(jnp.maximum(kt, 0) * tk, tk)
        kp = kt * tk + krow
        bias = jnp.where((kp > t1 - WINDOW) & (kp <= t1) & (kp >= 0), 0.0, NEG)
        s = _dot(kw_ref[pl.ds(off, tk), :], qt) + jnp.concatenate([bias] * NSA_HPG, axis=1)
        if d == 0:
            m_new = jnp.max(s, axis=0, keepdims=True)
            e = jnp.where(s > 0.5 * NEG, jnp.exp2(s - m_new), 0.0)
            carry = (m_new, jnp.sum(e, axis=0, keepdims=True), _dot(vwt_ref[:, pl.ds(off, tk)], e.astype(BF16)))
        else:
            carry = attend(s, vwt_ref[:, pl.ds(off, tk)], *carry)
    o_win = carry[2] / carry[1]

    gt = gate_ref[...].T
    eye_q = (lax.broadcasted_iota(jnp.int32, (tq, tq), 0)
             == lax.broadcasted_iota(jnp.int32, (tq, tq), 1)).astype(BF16)
    for h in range(NSA_HPG):
        cs = slice(h * tq, (h + 1) * tq)
        o = (gt[3 * h:3 * h + 1, :] * o_cmp[:, cs] + gt[3 * h + 1:3 * h + 2, :] * o_slc[:, cs]
             + gt[3 * h + 2:3 * h + 3, :] * o_win[:, cs])
        o_ref[h] = _transpose_bf16(eye_q, o.astype(BF16))


def _nsa(q_hm, cmp, kv4, gates, ovt, extn, batch, seq):
    t = q_hm.shape[1]
    tq, tk = NSA_TQ, NSA_TK
    assert tq % tk == 0 and (WINDOW + tq) % tk == 0
    nq = seq // tq
    ncmp = cmp.shape[2]
    qblk = pl.BlockSpec((NSA_HPG, tq, NSA_HEAD_DIM), lambda b, g, i: (g, b * nq + i, 0))
    cblk = lambda base: pl.BlockSpec((None, None, ncmp, NSA_HEAD_DIM), lambda b, g, i: (b, base + g, 0, 0))
    kvblk = lambda base: pl.BlockSpec((None, seq, NSA_HEAD_DIM), lambda b, g, i: (base + g, b, 0))
    return pl.pallas_call(
        functools.partial(_nsa_kernel, tq=tq, tk=tk),
        out_shape=jax.ShapeDtypeStruct((NSA_HEADS, t, NSA_HEAD_DIM), BF16),
        grid=(batch, NSA_KV_HEADS, nq),
        in_specs=[qblk, cblk(0), cblk(2), kvblk(0), kvblk(2), kvblk(4), kvblk(6),
                  pl.BlockSpec((tq, LANES), lambda b, g, i: (b * nq + i, g)),
                  _const_spec(ovt.shape), _const_spec(extn.shape)],
        out_specs=qblk,
        scratch_shapes=[pltpu.VMEM((seq, 2 * LANES), BF16), pltpu.VMEM((NSA_HEAD_DIM, seq), BF16),
                        pltpu.VMEM((NSA_HEAD_DIM, seq), BF16)],
        compiler_params=_params("arbitrary", "arbitrary", "arbitrary"),
        name="nsa",
    )(q_hm, cmp, cmp, kv4, kv4, kv4, kv4, gates, ovt, extn)


def _nsa2_kernel(q_ref, cmp_ref, kv_ref, gate_ref, ovt_ref, extn_ref, o_ref, kaug_ref, vst_ref, vwt_ref, *, tq):
    i = pl.program_id(1)
    tk = tq
    hd = NSA_HEAD_DIM
    m4 = NSA_HPG * tq
    seq = kv_ref.shape[1]
    n_sel = seq // SEL_BLOCK
    groups = range(NSA_KV_HEADS)
    eye = (lax.broadcasted_iota(jnp.int32, (hd, hd), 0)
           == lax.broadcasted_iota(jnp.int32, (hd, hd), 1)).astype(BF16)

    @pl.when(i == 0)
    def _():
        place = (lax.broadcasted_iota(jnp.int32, (hd, LANES), 0)
                 == lax.broadcasted_iota(jnp.int32, (hd, LANES), 1)).astype(BF16)
        for g in groups:
            kaug_ref[g, :, 0:LANES] = extn_ref[...]
            for kt in range(seq // tk):
                rs = slice(kt * tk, (kt + 1) * tk)
                kaug_ref[g, rs, LANES:2 * LANES] = _dot(kv_ref[g, rs, :], place).astype(BF16)
                vst_ref[g, :, rs] = _transpose_bf16(eye, kv_ref[2 + g, rs, :])
                vwt_ref[g, :, rs] = _transpose_bf16(eye, kv_ref[6 + g, rs, :])

    col = lax.broadcasted_iota(jnp.int32, (1, m4), 1)
    t4 = i * tq + jnp.bitwise_and(col, tq - 1)
    t1 = i * tq + lax.broadcasted_iota(jnp.int32, (1, tq), 1)
    krow = lax.broadcasted_iota(jnp.int32, (tk, 1), 0)
    m0 = jnp.full((1, m4), NEG, F32)
    l0 = jnp.zeros((1, m4), F32)
    acc0 = jnp.zeros((hd, m4), F32)

    def attend(s, vt_tile, m_run, l_run, acc):
        m_new = jnp.maximum(m_run, jnp.max(s, axis=0, keepdims=True))
        alpha = jnp.exp2(m_run - m_new)
        e = jnp.exp2(s - m_new)
        l_new = alpha * l_run + jnp.sum(e, axis=0, keepdims=True)
        return m_new, l_new, alpha * acc + _dot(vt_tile, e.astype(BF16))

    qt = [jnp.concatenate([_transpose_bf16(eye, q_ref[NSA_HPG * g + h]) for h in range(NSA_HPG)], axis=1)
          for g in groups]

    def window(g):
        carry = None
        for d in range(2):
            kt = i - 1 + d
            off = pl.multiple_of(jnp.maximum(kt, 0) * tk, tk)
            kp = kt * tk + krow
            bias = jnp.where((kp > t1 - WINDOW) & (kp <= t1) & (kp >= 0), 0.0, NEG)
            s = _dot(kv_ref[4 + g, pl.ds(off, tk), :], qt[g]) + jnp.concatenate([bias] * NSA_HPG, axis=1)
            vt_tile = vwt_ref[g, :, pl.ds(off, tk)]
            if d == 0:
                m_new = jnp.max(s, axis=0, keepdims=True)
                e = jnp.where(s > 0.5 * NEG, jnp.exp2(s - m_new), 0.0)
                carry = (m_new, jnp.sum(e, axis=0, keepdims=True), _dot(vt_tile, e.astype(BF16)))
            else:
                carry = attend(s, vt_tile, *carry)
        return carry[2] / carry[1]

    def compressed(g):
        ncmp = cmp_ref.shape[1]
        crow = lax.broadcasted_iota(jnp.int32, (ncmp, 1), 0)
        valid = (crow * CMP_STRIDE + (CMP_BLOCK - 1)) <= t4
        s = jnp.where(valid, _dot(cmp_ref[g], qt[g]), NEG)
        e = jnp.exp2(s - jnp.max(s, axis=0, keepdims=True))
        p = jnp.where(valid, e / jnp.sum(e, axis=0, keepdims=True), 0.0)
        o_cmp = _dot(_transpose_bf16(eye, cmp_ref[2 + g]), p.astype(BF16))
        psum = p[:, 0:tq] + p[:, tq:2 * tq] + p[:, 2 * tq:3 * tq] + p[:, 3 * tq:4 * tq]
        p_hi = psum.astype(BF16)
        p_lo = (psum - p_hi.astype(F32)).astype(BF16)
        imp = (_dot(ovt_ref[...], p_hi) + _dot(ovt_ref[...], p_lo))[0:n_sel, :]
        jrow = lax.broadcasted_iota(jnp.int32, (n_sel, 1), 0)
        cur = jnp.right_shift(t1, 6)
        forced = (jrow == 0) | (jrow == cur) | (jrow == cur - 1)
        vt = jnp.where(forced, 1e9, jnp.where(jrow * SEL_BLOCK <= t1, imp, -1.0))
        n_acc = 4
        cnt = [jnp.zeros((n_sel, tq), F32) for _ in range(n_acc)]
        for jp in range(n_sel):
            r = vt[jp:jp + 1, :]
            tie = jnp.where(jrow > jp, 1.0, 0.0)
            cnt[jp % n_acc] = cnt[jp % n_acc] + jnp.where(r > vt, 1.0, jnp.where(r == vt, tie, 0.0))
        rank = (cnt[0] + cnt[1]) + (cnt[2] + cnt[3])
        notsel = jnp.where(rank < float(SEL_TOPK), 0.0, 1.0)
        notsel = jnp.concatenate([notsel, jnp.zeros((LANES - n_sel, tq), F32)], axis=0).astype(BF16)
        qaug = jnp.concatenate([jnp.concatenate([notsel] * NSA_HPG, axis=1), qt[g], jnp.zeros((hd, m4), BF16)],
                               axis=0)
        return o_cmp, qaug

    o_win = [window(g) for g in groups]
    o_cmp, qaug = zip(*[compressed(g) for g in groups])

    def scores(g, kt):
        off = pl.multiple_of(kt * tk, tk)
        return _dot(kaug_ref[g, pl.ds(off, tk), :], qaug[g])

    def slc_body(kt, carry):
        out = []
        for g in groups:
            s_cur, m_run, l_run, acc = carry[g]
            s_next = scores(g, kt + 1)
            off = pl.multiple_of(kt * tk, tk)
            out.append((s_next,) + attend(s_cur, vst_ref[g, :, pl.ds(off, tk)], m_run, l_run, acc))
        return tuple(out)

    carry = lax.fori_loop(0, i, slc_body, tuple((scores(g, 0), m0, l0, acc0) for g in groups))
    causal = jnp.where((i * tk + krow) <= t1, 0.0, NEG)
    causal4 = jnp.concatenate([causal] * NSA_HPG, axis=1)
    off_d = pl.multiple_of(i * tk, tk)
    o_slc = []
    for g in groups:
        s_cur, m_run, l_run, acc = carry[g]
        _, l_fin, acc_fin = attend(s_cur + causal4, vst_ref[g, :, pl.ds(off_d, tk)], m_run, l_run, acc)
        o_slc.append(acc_fin / l_fin)

    eye_q = (lax.broadcasted_iota(jnp.int32, (tq, tq), 0)
             == lax.broadcasted_iota(jnp.int32, (tq, tq), 1)).astype(BF16)
    for g in groups:
        gt = gate_ref[:, g * LANES:(g + 1) * LANES].T
        for h in range(NSA_HPG):
            cs = slice(h * tq, (h + 1) * tq)
            o = (gt[3 * h:3 * h + 1, :] * o_cmp[g][:, cs] + gt[3 * h + 1:3 * h + 2, :] * o_slc[g][:, cs]
                 + gt[3 * h + 2:3 * h + 3, :] * o_win[g][:, cs])
            o_ref[NSA_HPG * g + h] = _transpose_bf16(eye_q, o.astype(BF16))


def _nsa2(q_hm, cmp, kv4, gates, ovt, extn, batch, seq):
    t = q_hm.shape[1]
    tq = NSA_TQ
    assert tq == WINDOW
    nq = seq // tq
    ncmp = cmp.shape[2]
    qblk = pl.BlockSpec((NSA_HEADS, tq, NSA_HEAD_DIM), lambda b, i: (0, b * nq + i, 0))
    return pl.pallas_call(
        functools.partial(_nsa2_kernel, tq=tq),
        out_shape=jax.ShapeDtypeStruct((NSA_HEADS, t, NSA_HEAD_DIM), BF16),
        grid=(batch, nq),
        in_specs=[qblk,
                  pl.BlockSpec((None, 4, ncmp, NSA_HEAD_DIM), lambda b, i: (b, 0, 0, 0)),
                  pl.BlockSpec((8, seq, NSA_HEAD_DIM), lambda b, i: (0, b, 0)),
                  pl.BlockSpec((tq, 2 * LANES), lambda b, i: (b * nq + i, 0)),
                  _const_spec(ovt.shape), _const_spec(extn.shape)],
        out_specs=qblk,
        scratch_shapes=[pltpu.VMEM((NSA_KV_HEADS, seq, 2 * LANES), BF16),
                        pltpu.VMEM((NSA_KV_HEADS, NSA_HEAD_DIM, seq), BF16),
                        pltpu.VMEM((NSA_KV_HEADS, NSA_HEAD_DIM, seq), BF16)],
        compiler_params=_params("arbitrary", "arbitrary"),
        name="nsa",
    )(q_hm, cmp, kv4, gates, ovt, extn)


def _merge_kernel(x_ref, pm_ref, on_ref, gm_ref, gpre_ref, wmg_ref, bmg_ref, wbp_ref, wbn_ref, wbg_ref,
                  wout_ref, gpost_ref, o_ref):
    x = x_ref[...]
    d = x.shape[1]
    h = _rms(x, gpre_ref[...]).astype(BF16)

    def gate(j):
        return jax.nn.sigmoid(_dot(h, wmg_ref[:, j * d:(j + 1) * d]) + bmg_ref[:, j * d:(j + 1) * d])

    y_nsa = _dot(on_ref[0], wbn_ref[0])
    for hh in range(1, NSA_HEADS):
        y_nsa = y_nsa + _dot(on_ref[hh], wbn_ref[hh])
    comb = gate(0) * _dot(pm_ref[...], wbp_ref[...])
    comb = comb + gate(1) * y_nsa
    comb = comb + gate(2) * _dot(gm_ref[...], wbg_ref[...])
    y = _dot(comb.astype(BF16), wout_ref[...])
    o_ref[...] = x + _rms(y, gpost_ref[...])


def _merge(x, pm, o_hm, gm, gpre, wmg, bmg, wbp, wbn, wbg, wout, gpost):
    t, d = x.shape
    tm = MERGE_TM
    row = lambda width: pl.BlockSpec((tm, width), lambda i: (i, 0))
    return pl.pallas_call(
        _merge_kernel,
        out_shape=jax.ShapeDtypeStruct((t, d), F32),
        grid=(t // tm,),
        in_specs=[row(d), row(POOL_WIDTH), pl.BlockSpec((NSA_HEADS, tm, NSA_HEAD_DIM), lambda i: (0, i, 0)),
                  row(GMLP_WIDTH), _const_spec((1, d)), _const_spec((d, 3 * d)), _const_spec((1, 3 * d)),
                  _const_spec((POOL_WIDTH, d)), _const_spec((NSA_HEADS, NSA_HEAD_DIM, d)),
                  _const_spec((GMLP_WIDTH, d)), _const_spec((d, d)), _const_spec((1, d))],
        out_specs=row(d),
        compiler_params=_params("parallel"),
        name="merge",
    )(x, pm, o_hm, gm, gpre, wmg, bmg, wbp, wbn, wbg, wout, gpost)


def _memkv_kernel(m_ref, g_ref, wk_ref, wv_ref, k_ref, v_ref):
    mn = _rms(m_ref[...], g_ref[...]).astype(BF16)
    k_ref[...] = _dot(mn, wk_ref[...]).astype(BF16)
    v_ref[...] = _dot(mn, wv_ref[...]).astype(BF16)


def _memkv(mem, g, wk, wv):
    batch, mlen, d = mem.shape
    blk = pl.BlockSpec((None, mlen, d), lambda b: (b, 0, 0))
    out = jax.ShapeDtypeStruct((batch, mlen, d), BF16)
    return pl.pallas_call(
        _memkv_kernel,
        out_shape=(out, out),
        grid=(batch,),
        in_specs=[blk, _const_spec((1, d)), _const_spec((d, d)), _const_spec((d, d))],
        out_specs=(blk, blk),
        compiler_params=_params("parallel"),
        name="memkv",
    )(mem, g, wk, wv)


def _memattn_kernel(x_ref, gpre_ref, wq_ref, k_ref, v_ref, wo_ref, gpost_ref, o_ref):
    x = x_ref[...]
    d = x.shape[1]
    hd = d // MEM_HEADS
    h = _rms(x, gpre_ref[...]).astype(BF16)
    q = (_dot(h, wq_ref[...]) * (hd ** -0.5)).astype(BF16)
    outs = []
    for n in range(MEM_HEADS):
        cs = slice(n * hd, (n + 1) * hd)
        s = _dot_nt(q[:, cs], k_ref[:, cs])
        e = jnp.exp(s - jnp.max(s, axis=-1, keepdims=True))
        p = e / jnp.sum(e, axis=-1, keepdims=True)
        outs.append(_dot(p.astype(BF16), v_ref[:, cs]).astype(BF16))
    y = _dot(jnp.concatenate(outs, axis=1), wo_ref[...])
    o_ref[...] = x + _rms(y, gpost_ref[...])


def _memattn(x, gpre, wq, k, v, wo, gpost, batch, seq):
    t, d = x.shape
    mlen = k.shape[1]
    tm = MEM_TM
    nt = seq // tm
    row = pl.BlockSpec((tm, d), lambda b, i: (b * nt + i, 0))
    kvb = pl.BlockSpec((None, mlen, d), lambda b, i: (b, 0, 0))
    return pl.pallas_call(
        _memattn_kernel,
        out_shape=jax.ShapeDtypeStruct((t, d), F32),
        grid=(batch, nt),
        in_specs=[row, _const_spec((1, d)), _const_spec((d, d)), kvb, kvb, _const_spec((d, d)),
                  _const_spec((1, d))],
        out_specs=row,
        compiler_params=_params("parallel", "arbitrary"),
        name="memattn",
    )(x, gpre, wq, k, v, wo, gpost)


def _overlap_matrix_t(n_cmp_pad, n_sel):
    cs = np.arange(n_cmp_pad)[None, :] * CMP_STRIDE
    ss = np.arange(LANES)[:, None] * SEL_BLOCK
    ov = np.minimum(cs + CMP_BLOCK, ss + SEL_BLOCK) - np.maximum(cs, ss)
    ov = np.maximum(ov, 0) / CMP_BLOCK
    ov[n_sel:, :] = 0.0
    return jnp.asarray(ov, dtype=BF16)


def _block_mask_matrix(seq):
    ex = (np.arange(seq)[:, None] // SEL_BLOCK) == np.arange(LANES)[None, :]
    return jnp.asarray(np.where(ex, -MASK_BIG, 0.0), dtype=BF16)


def kernel(x, mem, ff1_pre_g, ff1_w1, ff1_w3, ff1_w2, ff1_post_g, mix_pre_g, w_in, b_in, pool_w, pool_scale, cmp_pos_k, cmp_w1_k, cmp_w2_k, cmp_pos_v, cmp_w1_v, cmp_w2_v, gmlp_ln_g, gmlp_ln_b, gmlp_ws, gmlp_bs, w_br_pool, w_br_nsa, w_br_gmlp, w_mix_out, mix_post_g, mem_pre_g, mem_kv_g, mem_wq, mem_wk, mem_wv, mem_wo, mem_post_g, ff2_pre_g, ff2_w1, ff2_w3, ff2_w2, ff2_post_g):
    batch, seq, d = x.shape
    depth = w_in.shape[0]
    t = batch * seq
    n_cmp_rows = seq // CMP_STRIDE
    bf = lambda w: w.astype(BF16)
    vec = lambda g: g.reshape(depth, 1, -1)

    c_g = 1792
    c_gm = c_g + 3 * NSA_HEADS
    c_mg = c_gm + 2 * GMLP_WIDTH
    zeros_w = lambda n: jnp.zeros((depth, d, n), w_in.dtype)
    zeros_b = lambda n: jnp.zeros((depth, n), b_in.dtype)
    half = 3 * NSA_HPG
    w_proj = bf(jnp.concatenate(
        [w_in[:, :, :c_g], w_in[:, :, c_g:c_g + half], zeros_w(LANES - half),
         w_in[:, :, c_g + half:c_gm], zeros_w(LANES - half), w_in[:, :, c_gm:c_mg]], axis=2))
    b_proj = jnp.concatenate(
        [b_in[:, :c_g], b_in[:, c_g:c_g + half], zeros_b(LANES - half),
         b_in[:, c_g + half:c_gm], zeros_b(LANES - half), b_in[:, c_gm:c_mg]], axis=1).reshape(depth, 1, -1)
    w_mg = bf(w_in[:, :, c_mg:])
    b_mg = b_in[:, c_mg:].reshape(depth, 1, -1)
    bsx = jnp.broadcast_to(jnp.swapaxes(gmlp_bs, 1, 2)[:, :, :, None],
                           (depth, GMLP_CHUNK, GMLP_GROUPS, GMLP_WIDTH // GMLP_GROUPS)).reshape(
                               depth, GMLP_CHUNK, GMLP_WIDTH)
    half_w = CMP_STRIDE * NSA_HEAD_DIM
    cmp_pos = jnp.stack([cmp_pos_k, cmp_pos_v], axis=1).reshape(depth, 2, 2, half_w)
    cmp_w1 = bf(jnp.stack([cmp_w1_k, cmp_w1_v], axis=1)).reshape(depth, 2, 2, half_w, -1)
    cmp_w2 = bf(jnp.stack([cmp_w2_k, cmp_w2_v], axis=1))
    w_br_nsa_h = bf(w_br_nsa).reshape(depth, NSA_HEADS, NSA_HEAD_DIM, d)
    ff1 = (vec(ff1_pre_g), bf(ff1_w1), bf(ff1_w3), bf(ff1_w2), vec(ff1_post_g))
    ff2 = (vec(ff2_pre_g), bf(ff2_w1), bf(ff2_w3), bf(ff2_w2), vec(ff2_post_g))
    pool_w_b, pool_scale_v = bf(pool_w), vec(pool_scale)
    w_br_pool_b, w_br_gmlp_b, w_mix_out_b = bf(w_br_pool), bf(w_br_gmlp), bf(w_mix_out)
    mem_wq_b, mem_wk_b, mem_wv_b, mem_wo_b = bf(mem_wq), bf(mem_wk), bf(mem_wv), bf(mem_wo)
    mix_pre, mix_post = vec(mix_pre_g), vec(mix_post_g)
    mem_pre, mem_kv, mem_post = vec(mem_pre_g), vec(mem_kv_g), vec(mem_post_g)
    ln_g, ln_b = vec(gmlp_ln_g), vec(gmlp_ln_b)
    ov = _overlap_matrix_t(n_cmp_rows, seq // SEL_BLOCK)
    ex = _block_mask_matrix(seq)

    xs = x.reshape(t, d)
    for l in range(depth):
        xs = _ffn(xs, *(p[l] for p in ff1))
        a, q_hm, kvc, kv4, gates, gm = _inproj(xs, mix_pre[l], w_proj[l], b_proj[l], gmlp_ws[l], ln_g[l],
                                               ln_b[l], bsx[l])
        pm = _pool(a, pool_w_b[l], pool_scale_v[l], batch, seq)
        a4 = kvc.reshape(batch, n_cmp_rows, CMP_STRIDE, 4, NSA_HEAD_DIM).transpose(0, 3, 1, 2, 4).reshape(
            batch, 4, n_cmp_rows, half_w)
        cmp = _cmpkv(a4, cmp_pos[l], cmp_w1[l], cmp_w2[l])
        o_hm = _nsa2(q_hm, cmp, kv4, gates, ov, ex, batch, seq)
        xs = _merge(xs, pm, o_hm, gm, mix_pre[l], w_mg[l], b_mg[l], w_br_pool_b[l], w_br_nsa_h[l],
                    w_br_gmlp_b[l], w_mix_out_b[l], mix_post[l])
        mk, mv = _memkv(mem, mem_kv[l], mem_wk_b[l], mem_wv_b[l])
        xs = _memattn(xs, mem_pre[l], mem_wq_b[l], mk, mv, mem_wo_b[l], mem_post[l], batch, seq)
        xs = _ffn(xs, *(p[l] for p in ff2))
    return xs.reshape(batch, seq, d)
```

```python
import functools

import numpy as np
import jax
import jax.numpy as jnp
from jax import lax
from jax.experimental import pallas as pl
from jax.experimental.pallas import tpu as pltpu

F32 = jnp.float32
BF16 = jnp.bfloat16

EPS = 1e-6
NEG = -1e30
MASK_BIG = 2.0 ** 100
LOG2E = float(np.log2(np.e))

MEM_HEADS = 4
POOL_WINDOWS = (2, 4, 8, 16)
POOL_GROUP = 128
NSA_HEADS = 8
NSA_KV_HEADS = 2
NSA_HPG = NSA_HEADS // NSA_KV_HEADS
NSA_HEAD_DIM = 64
CMP_BLOCK = 32
CMP_STRIDE = 16
SEL_BLOCK = 64
SEL_TOPK = 8
WINDOW = 256
GMLP_WIDTH = 512
GMLP_GROUPS = 4
GMLP_CHUNK = 128
POOL_WIDTH = 512
NSA_WIDTH = NSA_HEADS * NSA_HEAD_DIM

LANES = 128
VMEM_LIMIT_BYTES = 56 * 1024 * 1024

FFN_TM = 512
FFN_CHUNK = 512
PROJ_TM = 256
MERGE_TM = 256
MEM_TM = 512
NSA_TQ = 256


def _dot(a, b):
    return jnp.dot(a, b, preferred_element_type=F32)


def _dot_nt(a, b):
    return lax.dot_general(a, b, (((1,), (1,)), ((), ())), preferred_element_type=F32)


def _rms(x, g):
    return x * lax.rsqrt(jnp.mean(x * x, axis=-1, keepdims=True) + EPS) * g


def _gelu(x):
    c = np.float32(np.sqrt(2.0 / np.pi))
    return x * (0.5 * (1.0 + jnp.tanh(c * (x + 0.044715 * (x * x * x)))))


def _const_spec(shape, layer=None):
    n = len(shape)
    if layer is None:
        return pl.BlockSpec(shape, lambda *_: (0,) * n, pipeline_mode=pl.Buffered(1))
    return pl.BlockSpec((None,) + tuple(shape), lambda *_: (layer,) + (0,) * n, pipeline_mode=pl.Buffered(1))


def _params(*sem):
    return pltpu.CompilerParams(dimension_semantics=sem, vmem_limit_bytes=VMEM_LIMIT_BYTES)


def _ffn_kernel(x_ref, gpre_ref, w1_ref, w3_ref, w2_ref, gpost_ref, o_ref, h_ref, acc_ref, *, chunks):
    x = x_ref[...]
    h_ref[...] = _rms(x, gpre_ref[...]).astype(BF16)
    for n, (off, width) in enumerate(chunks):
        h = h_ref[...]
        a = _dot(h, w1_ref[:, off:off + width])
        b = _dot(h, w3_ref[:, off:off + width])
        u = (a * jax.nn.sigmoid(a) * b).astype(BF16)
        y = _dot(u, w2_ref[off:off + width, :])
        if n == 0:
            acc_ref[...] = y
        else:
            acc_ref[...] += y
    o_ref[...] = x + 0.5 * _rms(acc_ref[...], gpost_ref[...])


def _ffn(x, l, gpre, w1, w3, w2, gpost):
    t, d = x.shape
    f = w1.shape[-1]
    tm = FFN_TM
    cs = functools.partial(_const_spec, layer=l)
    chunks = tuple((off, min(FFN_CHUNK, f - off)) for off in range(0, f, FFN_CHUNK))
    row = pl.BlockSpec((tm, d), lambda i: (i, 0))
    return pl.pallas_call(
        functools.partial(_ffn_kernel, chunks=chunks),
        out_shape=jax.ShapeDtypeStruct((t, d), F32),
        grid=(t // tm,),
        in_specs=[row, cs((1, d)), cs((d, f)), cs((d, f)), cs((f, d)), cs((1, d))],
        out_specs=row,
        scratch_shapes=[pltpu.VMEM((tm, d), BF16), pltpu.VMEM((tm, d), F32)],
        compiler_params=_params("parallel"),
        name="ffn",
    )(x, gpre, w1, w3, w2, gpost)


_A0, _Q0, _KV0, _G0, _GM0, _PROJ_COLS = 0, 512, 1024, 1792, 2048, 3072


def _inproj_kernel(x_ref, g_ref, w_ref, b_ref, ws_ref, lng_ref, lnb_ref, bsx_ref,
                   a_ref, q_ref, kvc_ref, kv4_ref, gate_ref, gm_ref, *, tm):
    h = _rms(x_ref[...], g_ref[...]).astype(BF16)
    z = _dot(h, w_ref[...]) + b_ref[...]
    a_ref[...] = z[:, _A0:_A0 + POOL_WIDTH]
    zq = z[:, _Q0:_Q0 + NSA_WIDTH] * (NSA_HEAD_DIM ** -0.5 * LOG2E)
    for hh in range(NSA_HEADS):
        q_ref[hh] = zq[:, hh * 64:(hh + 1) * 64].astype(BF16)
    kvc_ref[...] = z[:, _KV0:_KV0 + 256]
    for r in range(8):
        c0 = _KV0 + 256 + r * 64
        kv4_ref[r] = z[:, c0:c0 + 64].astype(BF16)
    gate_ref[...] = jax.nn.sigmoid(z[:, _G0:_G0 + 256])
    gm = _gelu(z[:, _GM0:_GM0 + 2 * GMLP_WIDTH])
    u = gm[:, :GMLP_WIDTH]
    v = gm[:, GMLP_WIDTH:]
    mu = jnp.mean(v, axis=-1, keepdims=True)
    vc = v - mu
    var = jnp.mean(vc * vc, axis=-1, keepdims=True)
    vn = (vc * lax.rsqrt(var + EPS) * lng_ref[...] + lnb_ref[...]).astype(BF16)
    ri = lax.broadcasted_iota(jnp.int32, (GMLP_CHUNK, GMLP_CHUNK), 0)
    ci = lax.broadcasted_iota(jnp.int32, (GMLP_CHUNK, GMLP_CHUNK), 1)
    for g in range(GMLP_GROUPS):
        wg = jnp.where(ri >= ci, ws_ref[g], 0.0).astype(BF16)
        cs = slice(g * 128, (g + 1) * 128)
        for c in range(tm // GMLP_CHUNK):
            rs = slice(c * GMLP_CHUNK, (c + 1) * GMLP_CHUNK)
            s = _dot(wg, vn[rs, cs]) + bsx_ref[:, cs]
            gm_ref[rs, cs] = (u[rs, cs] * s).astype(BF16)


def _inproj(x, l, g, w, b, ws, lng, lnb, bsx):
    t, d = x.shape
    tm = PROJ_TM
    cs = functools.partial(_const_spec, layer=l)
    row = lambda width: pl.BlockSpec((tm, width), lambda i: (i, 0))
    heads = pl.BlockSpec((8, tm, 64), lambda i: (0, i, 0))
    return pl.pallas_call(
        functools.partial(_inproj_kernel, tm=tm),
        out_shape=(
            jax.ShapeDtypeStruct((t, POOL_WIDTH), F32),
            jax.ShapeDtypeStruct((8, t, 64), BF16),
            jax.ShapeDtypeStruct((t, 256), F32),
            jax.ShapeDtypeStruct((8, t, 64), BF16),
            jax.ShapeDtypeStruct((t, 256), F32),
            jax.ShapeDtypeStruct((t, GMLP_WIDTH), BF16),
        ),
        grid=(t // tm,),
        in_specs=[row(d), cs((1, d)), cs((d, _PROJ_COLS)), cs((1, _PROJ_COLS)),
                  cs((GMLP_GROUPS, GMLP_CHUNK, GMLP_CHUNK)), cs((1, GMLP_WIDTH)), cs((1, GMLP_WIDTH)),
                  cs((GMLP_CHUNK, GMLP_WIDTH))],
        out_specs=(row(POOL_WIDTH), heads, row(256), heads, row(256), row(GMLP_WIDTH)),
        compiler_params=_params("parallel"),
        name="inproj",
    )(x, g, w, b, ws, lng, lnb, bsx)


_POOL_PAD = 16


def _pool_kernel(a_ref, pw_ref, ps_ref, o_ref, pad_ref, *, seq):
    pos = lax.broadcasted_iota(jnp.int32, (seq, 1), 0).astype(F32) + 1.0
    pad_ref[0:_POOL_PAD, :] = jnp.zeros((_POOL_PAD, POOL_GROUP), F32)
    for gi, w in enumerate(POOL_WINDOWS):
        cs = slice(gi * POOL_GROUP, (gi + 1) * POOL_GROUP)
        x = a_ref[:, cs]
        s = x
        k = 1
        while k < w:
            pad_ref[_POOL_PAD:_POOL_PAD + seq, :] = s
            s = s + pad_ref[_POOL_PAD - k:_POOL_PAD - k + seq, :]
            k *= 2
        pooled = (s / jnp.minimum(pos, float(w)) - x).astype(BF16)
        o_ref[:, cs] = (_dot(pooled, pw_ref[gi]) * ps_ref[:, cs]).astype(BF16)


def _pool(a, l, pw, ps, batch, seq):
    t = a.shape[0]
    blk = pl.BlockSpec((seq, POOL_WIDTH), lambda b: (b, 0))
    cs = functools.partial(_const_spec, layer=l)
    return pl.pallas_call(
        functools.partial(_pool_kernel, seq=seq),
        out_shape=jax.ShapeDtypeStruct((t, POOL_WIDTH), BF16),
        grid=(batch,),
        in_specs=[blk, cs((len(POOL_WINDOWS), POOL_GROUP, POOL_GROUP)), cs((1, POOL_WIDTH))],
        out_specs=blk,
        scratch_shapes=[pltpu.VMEM((_POOL_PAD + seq, POOL_GROUP), F32)],
        compiler_params=_params("parallel"),
        name="pool",
    )(a, pw, ps)


def _cmpkv_kernel(a_ref, pos_ref, w1_ref, w2_ref, o_ref):
    a = a_ref[...]
    xa = (a + pos_ref[0:1, :]).astype(BF16)
    xb = (a + pos_ref[1:2, :]).astype(BF16)
    p = _dot(xa, w1_ref[0])
    q = _dot(xb, w1_ref[1])
    nrow = q.shape[0]
    hpre = p + pltpu.roll(q, nrow - 1, axis=0)
    o_ref[...] = _dot(_gelu(hpre).astype(BF16), w2_ref[...]).astype(BF16)


def _cmpkv(a4, l, pos, w1, w2):
    batch, _, nrow, width = a4.shape
    hid = w1.shape[-1]
    return pl.pallas_call(
        _cmpkv_kernel,
        out_shape=jax.ShapeDtypeStruct((batch, 4, nrow, NSA_HEAD_DIM), BF16),
        grid=(batch, 4),
        in_specs=[pl.BlockSpec((None, None, nrow, width), lambda b, j: (b, j, 0, 0)),
                  pl.BlockSpec((None, None, 2, width), lambda b, j: (l, j // 2, 0, 0)),
                  pl.BlockSpec((None, None, 2, width, hid), lambda b, j: (l, j // 2, 0, 0, 0)),
                  pl.BlockSpec((None, None, hid, NSA_HEAD_DIM), lambda b, j: (l, j // 2, 0, 0))],
        out_specs=pl.BlockSpec((None, None, nrow, NSA_HEAD_DIM), lambda b, j: (b, j, 0, 0)),
        compiler_params=_params("parallel", "arbitrary"),
        name="cmpkv",
    )(a4, pos, w1, w2)


def _transpose_bf16(eye, x):
    return _dot_nt(eye, x).astype(BF16)


def _nsa_kernel(q_ref, cmp_ref, kv_ref, gate_ref, ovt_ref, extn_ref, o_ref, kaug_ref, vst_ref, vwt_ref, *, tq):
    i = pl.program_id(1)
    tk = tq
    hd = NSA_HEAD_DIM
    m4 = NSA_HPG * tq
    seq = kv_ref.shape[1]
    n_sel = seq // SEL_BLOCK
    groups = range(NSA_KV_HEADS)
    eye = (lax.broadcasted_iota(jnp.int32, (hd, hd), 0)
           == lax.broadcasted_iota(jnp.int32, (hd, hd), 1)).astype(BF16)

    @pl.when(i == 0)
    def _():
        place = (lax.broadcasted_iota(jnp.int32, (hd, LANES), 0)
                 == lax.broadcasted_iota(jnp.int32, (hd, LANES), 1)).astype(BF16)
        for g in groups:
            kaug_ref[g, :, 0:LANES] = extn_ref[...]
            for kt in range(seq // tk):
                rs = slice(kt * tk, (kt + 1) * tk)
                kaug_ref[g, rs, LANES:2 * LANES] = _dot(kv_ref[g, rs, :], place).astype(BF16)
                vst_ref[g, :, rs] = _transpose_bf16(eye, kv_ref[2 + g, rs, :])
                vwt_ref[g, :, rs] = _transpose_bf16(eye, kv_ref[6 + g, rs, :])

    col = lax.broadcasted_iota(jnp.int32, (1, m4), 1)
    t4 = i * tq + jnp.bitwise_and(col, tq - 1)
    t1 = i * tq + lax.broadcasted_iota(jnp.int32, (1, tq), 1)
    krow = lax.broadcasted_iota(jnp.int32, (tk, 1), 0)
    m0 = jnp.full((1, m4), NEG, F32)
    l0 = jnp.zeros((1, m4), F32)
    acc0 = jnp.zeros((hd, m4), F32)

    def attend(s, vt_tile, m_run, l_run, acc):
        m_new = jnp.maximum(m_run, jnp.max(s, axis=0, keepdims=True))
        alpha = jnp.exp2(m_run - m_new)
        e = jnp.exp2(s - m_new)
        l_new = alpha * l_run + jnp.sum(e, axis=0, keepdims=True)
        return m_new, l_new, alpha * acc + _dot(vt_tile, e.astype(BF16))

    qt = [jnp.concatenate([_transpose_bf16(eye, q_ref[NSA_HPG * g + h]) for h in range(NSA_HPG)], axis=1)
          for g in groups]

    def window(g):
        carry = None
        for d in range(2):
            kt = i - 1 + d
            off = pl.multiple_of(jnp.maximum(kt, 0) * tk, tk)
            kp = kt * tk + krow
            bias = jnp.where((kp > t1 - WINDOW) & (kp <= t1) & (kp >= 0), 0.0, NEG)
            s = _dot(kv_ref[4 + g, pl.ds(off, tk), :], qt[g]) + jnp.concatenate([bias] * NSA_HPG, axis=1)
            vt_tile = vwt_ref[g, :, pl.ds(off, tk)]
            if d == 0:
                m_new = jnp.max(s, axis=0, keepdims=True)
                e = jnp.where(s > 0.5 * NEG, jnp.exp2(s - m_new), 0.0)
                carry = (m_new, jnp.sum(e, axis=0, keepdims=True), _dot(vt_tile, e.astype(BF16)))
            else:
                carry = attend(s, vt_tile, *carry)
        return carry[2] / carry[1]

    def compressed(g):
        ncmp = cmp_ref.shape[1]
        crow = lax.broadcasted_iota(jnp.int32, (ncmp, 1), 0)
        valid = (crow * CMP_STRIDE + (CMP_BLOCK - 1)) <= t4
        s = jnp.where(valid, _dot(cmp_ref[g], qt[g]), NEG)
        e = jnp.exp2(s - jnp.max(s, axis=0, keepdims=True))
        p = jnp.where(valid, e / jnp.sum(e, axis=0, keepdims=True), 0.0)
        o_cmp = _dot(_transpose_bf16(eye, cmp_ref[2 + g]), p.astype(BF16))
        psum = p[:, 0:tq] + p[:, tq:2 * tq] + p[:, 2 * tq:3 * tq] + p[:, 3 * tq:4 * tq]
        p_hi = psum.astype(BF16)
        p_lo = (psum - p_hi.astype(F32)).astype(BF16)
        imp = (_dot(ovt_ref[...], p_hi) + _dot(ovt_ref[...], p_lo))[0:n_sel, :]
        jrow = lax.broadcasted_iota(jnp.int32, (n_sel, 1), 0)
        cur = jnp.right_shift(t1, 6)
        forced = (jrow == 0) | (jrow == cur) | (jrow == cur - 1)
        vt = jnp.where(forced, 1e9, jnp.where(jrow * SEL_BLOCK <= t1, imp, -1.0))
        n_acc = 4
        cnt = [jnp.zeros((n_sel, tq), F32) for _ in range(n_acc)]
        for jp in range(n_sel):
            r = vt[jp:jp + 1, :]
            tie = jnp.where(jrow > jp, 1.0, 0.0)
            cnt[jp % n_acc] = cnt[jp % n_acc] + jnp.where(r > vt, 1.0, jnp.where(r == vt, tie, 0.0))
        rank = (cnt[0] + cnt[1]) + (cnt[2] + cnt[3])
        notsel = jnp.where(rank < float(SEL_TOPK), 0.0, 1.0)
        notsel = jnp.concatenate([notsel, jnp.zeros((LANES - n_sel, tq), F32)], axis=0).astype(BF16)
        qaug = jnp.concatenate([jnp.concatenate([notsel] * NSA_HPG, axis=1), qt[g], jnp.zeros((hd, m4), BF16)],
                               axis=0)
        return o_cmp, qaug

    o_win = [window(g) for g in groups]
    o_cmp, qaug = zip(*[compressed(g) for g in groups])

    def scores(g, kt):
        off = pl.multiple_of(kt * tk, tk)
        return _dot(kaug_ref[g, pl.ds(off, tk), :], qaug[g])

    def slc_body(kt, carry):
        out = []
        for g in groups:
            s_cur, m_run, l_run, acc = carry[g]
            s_next = scores(g, kt + 1)
            off = pl.multiple_of(kt * tk, tk)
            out.append((s_next,) + attend(s_cur, vst_ref[g, :, pl.ds(off, tk)], m_run, l_run, acc))
        return tuple(out)

    carry = lax.fori_loop(0, i, slc_body, tuple((scores(g, 0), m0, l0, acc0) for g in groups))
    causal = jnp.where((i * tk + krow) <= t1, 0.0, NEG)
    causal4 = jnp.concatenate([causal] * NSA_HPG, axis=1)
    off_d = pl.multiple_of(i * tk, tk)
    o_slc = []
    for g in groups:
        s_cur, m_run, l_run, acc = carry[g]
        _, l_fin, acc_fin = attend(s_cur + causal4, vst_ref[g, :, pl.ds(off_d, tk)], m_run, l_run, acc)
        o_slc.append(acc_fin / l_fin)

    eye_q = (lax.broadcasted_iota(jnp.int32, (tq, tq), 0)
             == lax.broadcasted_iota(jnp.int32, (tq, tq), 1)).astype(BF16)
    o_heads = []
    for g in groups:
        gt = gate_ref[:, g * LANES:(g + 1) * LANES].T
        for h in range(NSA_HPG):
            cs = slice(h * tq, (h + 1) * tq)
            o = (gt[3 * h:3 * h + 1, :] * o_cmp[g][:, cs] + gt[3 * h + 1:3 * h + 2, :] * o_slc[g][:, cs]
                 + gt[3 * h + 2:3 * h + 3, :] * o_win[g][:, cs])
            o_heads.append(o.astype(BF16))
    o_ref[...] = _transpose_bf16(eye_q, jnp.concatenate(o_heads, axis=0))


def _nsa(q_hm, cmp, kv4, gates, ovt, extn, batch, seq):
    t = q_hm.shape[1]
    tq = NSA_TQ
    assert tq == WINDOW
    nq = seq // tq
    ncmp = cmp.shape[2]
    qblk = pl.BlockSpec((NSA_HEADS, tq, NSA_HEAD_DIM), lambda b, i: (0, b * nq + i, 0))
    return pl.pallas_call(
        functools.partial(_nsa_kernel, tq=tq),
        out_shape=jax.ShapeDtypeStruct((t, NSA_WIDTH), BF16),
        grid=(batch, nq),
        in_specs=[qblk,
                  pl.BlockSpec((None, 4, ncmp, NSA_HEAD_DIM), lambda b, i: (b, 0, 0, 0)),
                  pl.BlockSpec((8, seq, NSA_HEAD_DIM), lambda b, i: (0, b, 0)),
                  pl.BlockSpec((tq, 2 * LANES), lambda b, i: (b * nq + i, 0)),
                  _const_spec(ovt.shape), _const_spec(extn.shape)],
        out_specs=pl.BlockSpec((tq, NSA_WIDTH), lambda b, i: (b * nq + i, 0)),
        scratch_shapes=[pltpu.VMEM((NSA_KV_HEADS, seq, 2 * LANES), BF16),
                        pltpu.VMEM((NSA_KV_HEADS, NSA_HEAD_DIM, seq), BF16),
                        pltpu.VMEM((NSA_KV_HEADS, NSA_HEAD_DIM, seq), BF16)],
        compiler_params=_params("arbitrary", "arbitrary"),
        name="nsa",
    )(q_hm, cmp, kv4, gates, ovt, extn)


def _merge_kernel(x_ref, pm_ref, on_ref, gm_ref, gpre_ref, wmg_ref, bmg_ref, wbp_ref, wbn_ref, wbg_ref,
                  wout_ref, gpost_ref, o_ref):
    x = x_ref[...]
    d = x.shape[1]
    h = _rms(x, gpre_ref[...]).astype(BF16)

    def gate(j):
        return jax.nn.sigmoid(_dot(h, wmg_ref[:, j * d:(j + 1) * d]) + bmg_ref[:, j * d:(j + 1) * d])

    comb = gate(0) * _dot(pm_ref[...], wbp_ref[...])
    comb = comb + gate(1) * _dot(on_ref[...], wbn_ref[...])
    comb = comb + gate(2) * _dot(gm_ref[...], wbg_ref[...])
    y = _dot(comb.astype(BF16), wout_ref[...])
    o_ref[...] = x + _rms(y, gpost_ref[...])


def _merge(x, l, pm, o_nsa, gm, gpre, wmg, bmg, wbp, wbn, wbg, wout, gpost):
    t, d = x.shape
    tm = MERGE_TM
    row = lambda width: pl.BlockSpec((tm, width), lambda i: (i, 0))
    cs = functools.partial(_const_spec, layer=l)
    return pl.pallas_call(
        _merge_kernel,
        out_shape=jax.ShapeDtypeStruct((t, d), F32),
        grid=(t // tm,),
        in_specs=[row(d), row(POOL_WIDTH), row(NSA_WIDTH), row(GMLP_WIDTH), cs((1, d)), cs((d, 3 * d)),
                  cs((1, 3 * d)), cs((POOL_WIDTH, d)), cs((NSA_WIDTH, d)), cs((GMLP_WIDTH, d)), cs((d, d)),
                  cs((1, d))],
        out_specs=row(d),
        compiler_params=_params("parallel"),
        name="merge",
    )(x, pm, o_nsa, gm, gpre, wmg, bmg, wbp, wbn, wbg, wout, gpost)


def _memkv_kernel(m_ref, g_ref, wk_ref, wv_ref, k_ref, v_ref):
    mn = _rms(m_ref[...], g_ref[...]).astype(BF16)
    k_ref[...] = _dot(mn, wk_ref[...]).astype(BF16)
    v_ref[...] = _dot(mn, wv_ref[...]).astype(BF16)


def _memkv(mem, l, g, wk, wv):
    batch, mlen, d = mem.shape
    blk = pl.BlockSpec((None, mlen, d), lambda b: (b, 0, 0))
    out = jax.ShapeDtypeStruct((batch, mlen, d), BF16)
    cs = functools.partial(_const_spec, layer=l)
    return pl.pallas_call(
        _memkv_kernel,
        out_shape=(out, out),
        grid=(batch,),
        in_specs=[blk, cs((1, d)), cs((d, d)), cs((d, d))],
        out_specs=(blk, blk),
        compiler_params=_params("parallel"),
        name="memkv",
    )(mem, g, wk, wv)


def _memattn_kernel(x_ref, gpre_ref, wq_ref, k_ref, v_ref, wo_ref, gpost_ref, o_ref):
    x = x_ref[...]
    d = x.shape[1]
    hd = d // MEM_HEADS
    h = _rms(x, gpre_ref[...]).astype(BF16)
    q = (_dot(h, wq_ref[...]) * (hd ** -0.5)).astype(BF16)
    outs = []
    for n in range(MEM_HEADS):
        cs = slice(n * hd, (n + 1) * hd)
        s = _dot_nt(q[:, cs], k_ref[:, cs])
        e = jnp.exp(s - jnp.max(s, axis=-1, keepdims=True))
        p = e / jnp.sum(e, axis=-1, keepdims=True)
        outs.append(_dot(p.astype(BF16), v_ref[:, cs]).astype(BF16))
    y = _dot(jnp.concatenate(outs, axis=1), wo_ref[...])
    o_ref[...] = x + _rms(y, gpost_ref[...])


def _memattn(x, l, gpre, wq, k, v, wo, gpost, batch, seq):
    t, d = x.shape
    mlen = k.shape[1]
    tm = MEM_TM
    nt = seq // tm
    row = pl.BlockSpec((tm, d), lambda b, i: (b * nt + i, 0))
    kvb = pl.BlockSpec((None, mlen, d), lambda b, i: (b, 0, 0))
    cs = functools.partial(_const_spec, layer=l)
    return pl.pallas_call(
        _memattn_kernel,
        out_shape=jax.ShapeDtypeStruct((t, d), F32),
        grid=(batch, nt),
        in_specs=[row, cs((1, d)), cs((d, d)), kvb, kvb, cs((d, d)), cs((1, d))],
        out_specs=row,
        compiler_params=_params("parallel", "arbitrary"),
        name="memattn",
    )(x, gpre, wq, k, v, wo, gpost)


def _overlap_matrix_t(n_cmp_pad, n_sel):
    cs = np.arange(n_cmp_pad)[None, :] * CMP_STRIDE
    ss = np.arange(LANES)[:, None] * SEL_BLOCK
    ov = np.minimum(cs + CMP_BLOCK, ss + SEL_BLOCK) - np.maximum(cs, ss)
    ov = np.maximum(ov, 0) / CMP_BLOCK
    ov[n_sel:, :] = 0.0
    return jnp.asarray(ov, dtype=BF16)


def _block_mask_matrix(seq):
    ex = (np.arange(seq)[:, None] // SEL_BLOCK) == np.arange(LANES)[None, :]
    return jnp.asarray(np.where(ex, -MASK_BIG, 0.0), dtype=BF16)


def kernel(x, mem, ff1_pre_g, ff1_w1, ff1_w3, ff1_w2, ff1_post_g, mix_pre_g, w_in, b_in, pool_w, pool_scale, cmp_pos_k, cmp_w1_k, cmp_w2_k, cmp_pos_v, cmp_w1_v, cmp_w2_v, gmlp_ln_g, gmlp_ln_b, gmlp_ws, gmlp_bs, w_br_pool, w_br_nsa, w_br_gmlp, w_mix_out, mix_post_g, mem_pre_g, mem_kv_g, mem_wq, mem_wk, mem_wv, mem_wo, mem_post_g, ff2_pre_g, ff2_w1, ff2_w3, ff2_w2, ff2_post_g):
    batch, seq, d = x.shape
    depth = w_in.shape[0]
    t = batch * seq
    n_cmp_rows = seq // CMP_STRIDE
    bf = lambda w: w.astype(BF16)
    vec = lambda g: g.reshape(depth, 1, -1)

    c_g = 1792
    c_gm = c_g + 3 * NSA_HEADS
    c_mg = c_gm + 2 * GMLP_WIDTH
    zeros_w = lambda n: jnp.zeros((depth, d, n), w_in.dtype)
    zeros_b = lambda n: jnp.zeros((depth, n), b_in.dtype)
    half = 3 * NSA_HPG
    w_proj = bf(jnp.concatenate(
        [w_in[:, :, :c_g], w_in[:, :, c_g:c_g + half], zeros_w(LANES - half),
         w_in[:, :, c_g + half:c_gm], zeros_w(LANES - half), w_in[:, :, c_gm:c_mg]], axis=2))
    b_proj = jnp.concatenate(
        [b_in[:, :c_g], b_in[:, c_g:c_g + half], zeros_b(LANES - half),
         b_in[:, c_g + half:c_gm], zeros_b(LANES - half), b_in[:, c_gm:c_mg]], axis=1).reshape(depth, 1, -1)
    w_mg = bf(w_in[:, :, c_mg:])
    b_mg = b_in[:, c_mg:].reshape(depth, 1, -1)
    bsx = jnp.broadcast_to(jnp.swapaxes(gmlp_bs, 1, 2)[:, :, :, None],
                           (depth, GMLP_CHUNK, GMLP_GROUPS, GMLP_WIDTH // GMLP_GROUPS)).reshape(
                               depth, GMLP_CHUNK, GMLP_WIDTH)
    half_w = CMP_STRIDE * NSA_HEAD_DIM
    cmp_pos = jnp.stack([cmp_pos_k, cmp_pos_v], axis=1).reshape(depth, 2, 2, half_w)
    cmp_w1 = bf(jnp.stack([cmp_w1_k, cmp_w1_v], axis=1)).reshape(depth, 2, 2, half_w, -1)
    cmp_w2 = bf(jnp.stack([cmp_w2_k, cmp_w2_v], axis=1))
    w_br_nsa_b = bf(w_br_nsa)
    ff1 = (vec(ff1_pre_g), bf(ff1_w1), bf(ff1_w3), bf(ff1_w2), vec(ff1_post_g))
    ff2 = (vec(ff2_pre_g), bf(ff2_w1), bf(ff2_w3), bf(ff2_w2), vec(ff2_post_g))
    pool_w_b, pool_scale_v = bf(pool_w), vec(pool_scale)
    w_br_pool_b, w_br_gmlp_b, w_mix_out_b = bf(w_br_pool), bf(w_br_gmlp), bf(w_mix_out)
    mem_wq_b, mem_wk_b, mem_wv_b, mem_wo_b = bf(mem_wq), bf(mem_wk), bf(mem_wv), bf(mem_wo)
    mix_pre, mix_post = vec(mix_pre_g), vec(mix_post_g)
    mem_pre, mem_kv, mem_post = vec(mem_pre_g), vec(mem_kv_g), vec(mem_post_g)
    ln_g, ln_b = vec(gmlp_ln_g), vec(gmlp_ln_b)
    ov = _overlap_matrix_t(n_cmp_rows, seq // SEL_BLOCK)
    ex = _block_mask_matrix(seq)

    xs = x.reshape(t, d)
    for l in range(depth):
        xs = _ffn(xs, l, *ff1)
        a, q_hm, kvc, kv4, gates, gm = _inproj(xs, l, mix_pre, w_proj, b_proj, gmlp_ws, ln_g, ln_b, bsx)
        pm = _pool(a, l, pool_w_b, pool_scale_v, batch, seq)
        a4 = kvc.reshape(batch, n_cmp_rows, CMP_STRIDE, 4, NSA_HEAD_DIM).transpose(0, 3, 1, 2, 4).reshape(
            batch, 4, n_cmp_rows, half_w)
        cmp = _cmpkv(a4, l, cmp_pos, cmp_w1, cmp_w2)
        o_nsa = _nsa(q_hm, cmp, kv4, gates, ov, ex, batch, seq)
        xs = _merge(xs, l, pm, o_nsa, gm, mix_pre, w_mg, b_mg, w_br_pool_b, w_br_nsa_b, w_br_gmlp_b,
                    w_mix_out_b, mix_post)
        mk, mv = _memkv(mem, l, mem_kv, mem_wk_b, mem_wv_b)
        xs = _memattn(xs, l, mem_pre, mem_wq_b, mk, mv, mem_wo_b, mem_post, batch, seq)
        xs = _ffn(xs, l, *ff2)
    return xs.reshape(batch, seq, d)
```

```python
import functools

import numpy as np
import jax
import jax.numpy as jnp
from jax import lax
from jax.experimental import pallas as pl
from jax.experimental.pallas import tpu as pltpu

F32 = jnp.float32
BF16 = jnp.bfloat16

EPS = 1e-6
NEG = -1e30
MASK_BIG = 2.0 ** 100
LOG2E = float(np.log2(np.e))

MEM_HEADS = 4
POOL_WINDOWS = (2, 4, 8, 16)
POOL_GROUP = 128
NSA_HEADS = 8
NSA_KV_HEADS = 2
NSA_HPG = NSA_HEADS // NSA_KV_HEADS
NSA_HEAD_DIM = 64
CMP_BLOCK = 32
CMP_STRIDE = 16
SEL_BLOCK = 64
SEL_TOPK = 8
WINDOW = 256
GMLP_WIDTH = 512
GMLP_GROUPS = 4
GMLP_CHUNK = 128
POOL_WIDTH = 512
NSA_WIDTH = NSA_HEADS * NSA_HEAD_DIM

LANES = 128
VMEM_LIMIT_BYTES = 56 * 1024 * 1024

FFN_TM = 512
FFN_CHUNK = 512
PROJ_TM = 512
MERGE_TM = 512
MEM_TM = 512
NSA_TQ = 256


def _dot(a, b):
    return jnp.dot(a, b, preferred_element_type=F32)


def _dot_nt(a, b):
    return lax.dot_general(a, b, (((1,), (1,)), ((), ())), preferred_element_type=F32)


def _rms(x, g):
    return x * lax.rsqrt(jnp.mean(x * x, axis=-1, keepdims=True) + EPS) * g


def _gelu(x):
    c = np.float32(np.sqrt(2.0 / np.pi))
    return x * (0.5 * (1.0 + jnp.tanh(c * (x + 0.044715 * (x * x * x)))))


def _const_spec(shape, layer=None):
    n = len(shape)
    if layer is None:
        return pl.BlockSpec(shape, lambda *_: (0,) * n, pipeline_mode=pl.Buffered(1))
    return pl.BlockSpec((None,) + tuple(shape), lambda *_: (layer,) + (0,) * n, pipeline_mode=pl.Buffered(1))


def _params(*sem):
    return pltpu.CompilerParams(dimension_semantics=sem, vmem_limit_bytes=VMEM_LIMIT_BYTES)


def _ffn_kernel(x_ref, gpre_ref, w1_ref, w3_ref, w2_ref, gpost_ref, o_ref, h_ref, acc_ref, *, chunks):
    x = x_ref[...]
    h_ref[...] = _rms(x, gpre_ref[...]).astype(BF16)
    for n, (off, width) in enumerate(chunks):
        h = h_ref[...]
        a = _dot(h, w1_ref[:, off:off + width])
        b = _dot(h, w3_ref[:, off:off + width])
        u = (a * jax.nn.sigmoid(a) * b).astype(BF16)
        y = _dot(u, w2_ref[off:off + width, :])
        if n == 0:
            acc_ref[...] = y
        else:
            acc_ref[...] += y
    o_ref[...] = x + 0.5 * _rms(acc_ref[...], gpost_ref[...])


def _ffn(x, l, gpre, w1, w3, w2, gpost):
    t, d = x.shape
    f = w1.shape[-1]
    tm = FFN_TM
    cs = functools.partial(_const_spec, layer=l)
    chunks = tuple((off, min(FFN_CHUNK, f - off)) for off in range(0, f, FFN_CHUNK))
    row = pl.BlockSpec((tm, d), lambda i: (i, 0))
    return pl.pallas_call(
        functools.partial(_ffn_kernel, chunks=chunks),
        out_shape=jax.ShapeDtypeStruct((t, d), F32),
        grid=(t // tm,),
        in_specs=[row, cs((1, d)), cs((d, f)), cs((d, f)), cs((f, d)), cs((1, d))],
        out_specs=row,
        scratch_shapes=[pltpu.VMEM((tm, d), BF16), pltpu.VMEM((tm, d), F32)],
        compiler_params=_params("parallel"),
        name="ffn",
    )(x, gpre, w1, w3, w2, gpost)


_A0, _Q0, _KV0, _G0, _GM0, _PROJ_COLS = 0, 512, 1024, 1792, 2048, 3072


def _inproj_kernel(x_ref, g_ref, w_ref, b_ref, ws_ref, lng_ref, lnb_ref, bsx_ref,
                   a_ref, q_ref, kvc_ref, kv4_ref, gate_ref, gm_ref, *, tm):
    h = _rms(x_ref[...], g_ref[...]).astype(BF16)
    z = _dot(h, w_ref[...]) + b_ref[...]
    a_ref[...] = z[:, _A0:_A0 + POOL_WIDTH]
    zq = z[:, _Q0:_Q0 + NSA_WIDTH] * (NSA_HEAD_DIM ** -0.5 * LOG2E)
    for hh in range(NSA_HEADS):
        q_ref[hh] = zq[:, hh * 64:(hh + 1) * 64].astype(BF16)
    kvc_ref[...] = z[:, _KV0:_KV0 + 256]
    for r in range(8):
        c0 = _KV0 + 256 + r * 64
        kv4_ref[r] = z[:, c0:c0 + 64].astype(BF16)
    gate_ref[...] = jax.nn.sigmoid(z[:, _G0:_G0 + 256])
    gm = _gelu(z[:, _GM0:_GM0 + 2 * GMLP_WIDTH])
    u = gm[:, :GMLP_WIDTH]
    v = gm[:, GMLP_WIDTH:]
    mu = jnp.mean(v, axis=-1, keepdims=True)
    vc = v - mu
    var = jnp.mean(vc * vc, axis=-1, keepdims=True)
    vn = (vc * lax.rsqrt(var + EPS) * lng_ref[...] + lnb_ref[...]).astype(BF16)
    ri = lax.broadcasted_iota(jnp.int32, (GMLP_CHUNK, GMLP_CHUNK), 0)
    ci = lax.broadcasted_iota(jnp.int32, (GMLP_CHUNK, GMLP_CHUNK), 1)
    for g in range(GMLP_GROUPS):
        wg = jnp.where(ri >= ci, ws_ref[g], 0.0).astype(BF16)
        cs = slice(g * 128, (g + 1) * 128)
        for c in range(tm // GMLP_CHUNK):
            rs = slice(c * GMLP_CHUNK, (c + 1) * GMLP_CHUNK)
            s = _dot(wg, vn[rs, cs]) + bsx_ref[:, cs]
            gm_ref[rs, cs] = (u[rs, cs] * s).astype(BF16)


def _inproj(x, l, g, w, b, ws, lng, lnb, bsx):
    t, d = x.shape
    tm = PROJ_TM
    cs = functools.partial(_const_spec, layer=l)
    row = lambda width: pl.BlockSpec((tm, width), lambda i: (i, 0))
    heads = pl.BlockSpec((8, tm, 64), lambda i: (0, i, 0))
    return pl.pallas_call(
        functools.partial(_inproj_kernel, tm=tm),
        out_shape=(
            jax.ShapeDtypeStruct((t, POOL_WIDTH), F32),
            jax.ShapeDtypeStruct((8, t, 64), BF16),
            jax.ShapeDtypeStruct((t, 256), F32),
            jax.ShapeDtypeStruct((8, t, 64), BF16),
            jax.ShapeDtypeStruct((t, 256), F32),
            jax.ShapeDtypeStruct((t, GMLP_WIDTH), BF16),
        ),
        grid=(t // tm,),
        in_specs=[row(d), cs((1, d)), cs((d, _PROJ_COLS)), cs((1, _PROJ_COLS)),
                  cs((GMLP_GROUPS, GMLP_CHUNK, GMLP_CHUNK)), cs((1, GMLP_WIDTH)), cs((1, GMLP_WIDTH)),
                  cs((GMLP_CHUNK, GMLP_WIDTH))],
        out_specs=(row(POOL_WIDTH), heads, row(256), heads, row(256), row(GMLP_WIDTH)),
        compiler_params=_params("parallel"),
        name="inproj",
    )(x, g, w, b, ws, lng, lnb, bsx)


_POOL_PAD = 16


def _pool_kernel(a_ref, pw_ref, ps_ref, o_ref, pad_ref, *, seq):
    pos = lax.broadcasted_iota(jnp.int32, (seq, 1), 0).astype(F32) + 1.0
    pad_ref[0:_POOL_PAD, :] = jnp.zeros((_POOL_PAD, POOL_GROUP), F32)
    for gi, w in enumerate(POOL_WINDOWS):
        cs = slice(gi * POOL_GROUP, (gi + 1) * POOL_GROUP)
        x = a_ref[:, cs]
        s = x
        k = 1
        while k < w:
            pad_ref[_POOL_PAD:_POOL_PAD + seq, :] = s
            s = s + pad_ref[_POOL_PAD - k:_POOL_PAD - k + seq, :]
            k *= 2
        pooled = (s / jnp.minimum(pos, float(w)) - x).astype(BF16)
        o_ref[:, cs] = (_dot(pooled, pw_ref[gi]) * ps_ref[:, cs]).astype(BF16)


def _pool(a, l, pw, ps, batch, seq):
    t = a.shape[0]
    blk = pl.BlockSpec((seq, POOL_WIDTH), lambda b: (b, 0))
    cs = functools.partial(_const_spec, layer=l)
    return pl.pallas_call(
        functools.partial(_pool_kernel, seq=seq),
        out_shape=jax.ShapeDtypeStruct((t, POOL_WIDTH), BF16),
        grid=(batch,),
        in_specs=[blk, cs((len(POOL_WINDOWS), POOL_GROUP, POOL_GROUP)), cs((1, POOL_WIDTH))],
        out_specs=blk,
        scratch_shapes=[pltpu.VMEM((_POOL_PAD + seq, POOL_GROUP), F32)],
        compiler_params=_params("parallel"),
        name="pool",
    )(a, pw, ps)


def _cmpkv_kernel(a_ref, pos_ref, w1_ref, w2_ref, o_ref):
    a = a_ref[...]
    xa = (a + pos_ref[0:1, :]).astype(BF16)
    xb = (a + pos_ref[1:2, :]).astype(BF16)
    p = _dot(xa, w1_ref[0])
    q = _dot(xb, w1_ref[1])
    nrow = q.shape[0]
    hpre = p + pltpu.roll(q, nrow - 1, axis=0)
    o_ref[...] = _dot(_gelu(hpre).astype(BF16), w2_ref[...]).astype(BF16)


def _cmpkv(a4, l, pos, w1, w2):
    batch, _, nrow, width = a4.shape
    hid = w1.shape[-1]
    return pl.pallas_call(
        _cmpkv_kernel,
        out_shape=jax.ShapeDtypeStruct((batch, 4, nrow, NSA_HEAD_DIM), BF16),
        grid=(batch, 4),
        in_specs=[pl.BlockSpec((None, None, nrow, width), lambda b, j: (b, j, 0, 0)),
                  pl.BlockSpec((None, None, 2, width), lambda b, j: (l, j // 2, 0, 0)),
                  pl.BlockSpec((None, None, 2, width, hid), lambda b, j: (l, j // 2, 0, 0, 0)),
                  pl.BlockSpec((None, None, hid, NSA_HEAD_DIM), lambda b, j: (l, j // 2, 0, 0))],
        out_specs=pl.BlockSpec((None, None, nrow, NSA_HEAD_DIM), lambda b, j: (b, j, 0, 0)),
        compiler_params=_params("parallel", "arbitrary"),
        name="cmpkv",
    )(a4, pos, w1, w2)


def _transpose_bf16(eye, x):
    return _dot_nt(eye, x).astype(BF16)


def _nsa_kernel(q_ref, cmp_ref, kv_ref, gate_ref, ovt_ref, extn_ref, o_ref, kaug_ref, vst_ref, vwt_ref, *, tq):
    i = pl.program_id(1)
    tk = tq
    hd = NSA_HEAD_DIM
    m4 = NSA_HPG * tq
    seq = kv_ref.shape[1]
    n_sel = seq // SEL_BLOCK
    groups = range(NSA_KV_HEADS)
    eye = (lax.broadcasted_iota(jnp.int32, (hd, hd), 0)
           == lax.broadcasted_iota(jnp.int32, (hd, hd), 1)).astype(BF16)

    @pl.when(i == 0)
    def _():
        place = (lax.broadcasted_iota(jnp.int32, (hd, LANES), 0)
                 == lax.broadcasted_iota(jnp.int32, (hd, LANES), 1)).astype(BF16)
        for g in groups:
            kaug_ref[g, :, 0:LANES] = extn_ref[...]
            for kt in range(seq // tk):
                rs = slice(kt * tk, (kt + 1) * tk)
                kaug_ref[g, rs, LANES:2 * LANES] = _dot(kv_ref[g, rs, :], place).astype(BF16)
                vst_ref[g, :, rs] = _transpose_bf16(eye, kv_ref[2 + g, rs, :])
                vwt_ref[g, :, rs] = _transpose_bf16(eye, kv_ref[6 + g, rs, :])

    col = lax.broadcasted_iota(jnp.int32, (1, m4), 1)
    t4 = i * tq + jnp.bitwise_and(col, tq - 1)
    t1 = i * tq + lax.broadcasted_iota(jnp.int32, (1, tq), 1)
    krow = lax.broadcasted_iota(jnp.int32, (tk, 1), 0)
    m0 = jnp.full((1, m4), NEG, F32)
    l0 = jnp.zeros((1, m4), F32)
    acc0 = jnp.zeros((hd, m4), F32)

    def attend(s, vt_tile, m_run, l_run, acc):
        m_new = jnp.maximum(m_run, jnp.max(s, axis=0, keepdims=True))
        alpha = jnp.exp2(m_run - m_new)
        e = jnp.exp2(s - m_new)
        l_new = alpha * l_run + jnp.sum(e, axis=0, keepdims=True)
        return m_new, l_new, alpha * acc + _dot(vt_tile, e.astype(BF16))

    qt = [jnp.concatenate([_transpose_bf16(eye, q_ref[NSA_HPG * g + h]) for h in range(NSA_HPG)], axis=1)
          for g in groups]

    off_w = pl.multiple_of(jnp.maximum(i - 1, 0) * tk, tk)
    kp_w = off_w + lax.broadcasted_iota(jnp.int32, (2 * tk, 1), 0)
    bias_w = jnp.where((kp_w > t1 - WINDOW) & (kp_w <= t1), 0.0, NEG)
    bias_w4 = jnp.concatenate([bias_w] * NSA_HPG, axis=1)

    def window(g):
        s = _dot(kv_ref[4 + g, pl.ds(off_w, 2 * tk), :], qt[g]) + bias_w4
        e = jnp.exp2(s - jnp.max(s, axis=0, keepdims=True))
        acc = _dot(vwt_ref[g, :, pl.ds(off_w, 2 * tk)], e.astype(BF16))
        return acc / jnp.sum(e, axis=0, keepdims=True)

    def compressed(g):
        ncmp = cmp_ref.shape[1]
        crow = lax.broadcasted_iota(jnp.int32, (ncmp, 1), 0)
        valid = (crow * CMP_STRIDE + (CMP_BLOCK - 1)) <= t4
        s = jnp.where(valid, _dot(cmp_ref[g], qt[g]), NEG)
        e = jnp.exp2(s - jnp.max(s, axis=0, keepdims=True))
        p = jnp.where(valid, e / jnp.sum(e, axis=0, keepdims=True), 0.0)
        o_cmp = _dot(_transpose_bf16(eye, cmp_ref[2 + g]), p.astype(BF16))
        psum = p[:, 0:tq] + p[:, tq:2 * tq] + p[:, 2 * tq:3 * tq] + p[:, 3 * tq:4 * tq]
        p_hi = psum.astype(BF16)
        p_lo = (psum - p_hi.astype(F32)).astype(BF16)
        imp = (_dot(ovt_ref[...], p_hi) + _dot(ovt_ref[...], p_lo))[0:n_sel, :]
        jrow = lax.broadcasted_iota(jnp.int32, (n_sel, 1), 0)
        cur = jnp.right_shift(t1, 6)
        forced = (jrow == 0) | (jrow == cur) | (jrow == cur - 1)
        vt = jnp.where(forced, 1e9, jnp.where(jrow * SEL_BLOCK <= t1, imp, -1.0))
        n_acc = 4
        cnt = [jnp.zeros((n_sel, tq), F32) for _ in range(n_acc)]
        for jp in range(n_sel):
            r = vt[jp:jp + 1, :]
            tie = jnp.where(jrow > jp, 1.0, 0.0)
            cnt[jp % n_acc] = cnt[jp % n_acc] + jnp.where(r > vt, 1.0, jnp.where(r == vt, tie, 0.0))
        rank = (cnt[0] + cnt[1]) + (cnt[2] + cnt[3])
        notsel = jnp.where(rank < float(SEL_TOPK), 0.0, 1.0)
        notsel = jnp.concatenate([notsel, jnp.zeros((LANES - n_sel, tq), F32)], axis=0).astype(BF16)
        qaug = jnp.concatenate([jnp.concatenate([notsel] * NSA_HPG, axis=1), qt[g], jnp.zeros((hd, m4), BF16)],
                               axis=0)
        return o_cmp, qaug

    o_win = [window(g) for g in groups]
    o_cmp, qaug = zip(*[compressed(g) for g in groups])

    def scores(g, kt):
        off = pl.multiple_of(kt * tk, tk)
        return _dot(kaug_ref[g, pl.ds(off, tk), :], qaug[g])

    def slc_body(kt, carry):
        off = pl.multiple_of(kt * tk, tk)
        return tuple(attend(scores(g, kt), vst_ref[g, :, pl.ds(off, tk)], *carry[g]) for g in groups)

    carry = lax.fori_loop(0, i, slc_body, tuple((m0, l0, acc0) for g in groups))
    causal = jnp.where((i * tk + krow) <= t1, 0.0, NEG)
    causal4 = jnp.concatenate([causal] * NSA_HPG, axis=1)
    off_d = pl.multiple_of(i * tk, tk)
    o_slc = []
    for g in groups:
        _, l_fin, acc_fin = attend(scores(g, i) + causal4, vst_ref[g, :, pl.ds(off_d, tk)], *carry[g])
        o_slc.append(acc_fin / l_fin)

    eye_q = (lax.broadcasted_iota(jnp.int32, (tq, tq), 0)
             == lax.broadcasted_iota(jnp.int32, (tq, tq), 1)).astype(BF16)
    o_heads = []
    for g in groups:
        gt = gate_ref[:, g * LANES:(g + 1) * LANES].T
        for h in range(NSA_HPG):
            cs = slice(h * tq, (h + 1) * tq)
            o = (gt[3 * h:3 * h + 1, :] * o_cmp[g][:, cs] + gt[3 * h + 1:3 * h + 2, :] * o_slc[g][:, cs]
                 + gt[3 * h + 2:3 * h + 3, :] * o_win[g][:, cs])
            o_heads.append(o.astype(BF16))
    o_ref[...] = _transpose_bf16(eye_q, jnp.concatenate(o_heads, axis=0))


def _nsa(q_hm, cmp, kv4, gates, ovt, extn, batch, seq):
    t = q_hm.shape[1]
    tq = NSA_TQ
    assert tq == WINDOW
    nq = seq // tq
    ncmp = cmp.shape[2]
    qblk = pl.BlockSpec((NSA_HEADS, tq, NSA_HEAD_DIM), lambda b, i: (0, b * nq + i, 0))
    return pl.pallas_call(
        functools.partial(_nsa_kernel, tq=tq),
        out_shape=jax.ShapeDtypeStruct((t, NSA_WIDTH), BF16),
        grid=(batch, nq),
        in_specs=[qblk,
                  pl.BlockSpec((None, 4, ncmp, NSA_HEAD_DIM), lambda b, i: (b, 0, 0, 0)),
                  pl.BlockSpec((8, seq, NSA_HEAD_DIM), lambda b, i: (0, b, 0)),
                  pl.BlockSpec((tq, 2 * LANES), lambda b, i: (b * nq + i, 0)),
                  _const_spec(ovt.shape), _const_spec(extn.shape)],
        out_specs=pl.BlockSpec((tq, NSA_WIDTH), lambda b, i: (b * nq + i, 0)),
        scratch_shapes=[pltpu.VMEM((NSA_KV_HEADS, seq, 2 * LANES), BF16),
                        pltpu.VMEM((NSA_KV_HEADS, NSA_HEAD_DIM, seq), BF16),
                        pltpu.VMEM((NSA_KV_HEADS, NSA_HEAD_DIM, seq), BF16)],
        compiler_params=_params("arbitrary", "arbitrary"),
        name="nsa",
    )(q_hm, cmp, kv4, gates, ovt, extn)


def _merge_kernel(x_ref, pm_ref, on_ref, gm_ref, gpre_ref, wmg_ref, bmg_ref, wbp_ref, wbn_ref, wbg_ref,
                  wout_ref, gpost_ref, o_ref):
    x = x_ref[...]
    d = x.shape[1]
    h = _rms(x, gpre_ref[...]).astype(BF16)

    def gate(j):
        return jax.nn.sigmoid(_dot(h, wmg_ref[:, j * d:(j + 1) * d]) + bmg_ref[:, j * d:(j + 1) * d])

    comb = gate(0) * _dot(pm_ref[...], wbp_ref[...])
    comb = comb + gate(1) * _dot(on_ref[...], wbn_ref[...])
    comb = comb + gate(2) * _dot(gm_ref[...], wbg_ref[...])
    y = _dot(comb.astype(BF16), wout_ref[...])
    o_ref[...] = x + _rms(y, gpost_ref[...])


def _merge(x, l, pm, o_nsa, gm, gpre, wmg, bmg, wbp, wbn, wbg, wout, gpost):
    t, d = x.shape
    tm = MERGE_TM
    row = lambda width: pl.BlockSpec((tm, width), lambda i: (i, 0))
    cs = functools.partial(_const_spec, layer=l)
    return pl.pallas_call(
        _merge_kernel,
        out_shape=jax.ShapeDtypeStruct((t, d), F32),
        grid=(t // tm,),
        in_specs=[row(d), row(POOL_WIDTH), row(NSA_WIDTH), row(GMLP_WIDTH), cs((1, d)), cs((d, 3 * d)),
                  cs((1, 3 * d)), cs((POOL_WIDTH, d)), cs((NSA_WIDTH, d)), cs((GMLP_WIDTH, d)), cs((d, d)),
                  cs((1, d))],
        out_specs=row(d),
        compiler_params=_params("parallel"),
        name="merge",
    )(x, pm, o_nsa, gm, gpre, wmg, bmg, wbp, wbn, wbg, wout, gpost)


def _memkv_kernel(m_ref, g_ref, wk_ref, wv_ref, k_ref, v_ref):
    mn = _rms(m_ref[...], g_ref[...]).astype(BF16)
    k_ref[...] = _dot(mn, wk_ref[...]).astype(BF16)
    v_ref[...] = _dot(mn, wv_ref[...]).astype(BF16)


def _memkv(mem, l, g, wk, wv):
    batch, mlen, d = mem.shape
    blk = pl.BlockSpec((None, mlen, d), lambda b: (b, 0, 0))
    out = jax.ShapeDtypeStruct((batch, mlen, d), BF16)
    cs = functools.partial(_const_spec, layer=l)
    return pl.pallas_call(
        _memkv_kernel,
        out_shape=(out, out),
        grid=(batch,),
        in_specs=[blk, cs((1, d)), cs((d, d)), cs((d, d))],
        out_specs=(blk, blk),
        compiler_params=_params("parallel"),
        name="memkv",
    )(mem, g, wk, wv)


def _memattn_kernel(x_ref, gpre_ref, wq_ref, k_ref, v_ref, wo_ref, gpost_ref, o_ref):
    x = x_ref[...]
    d = x.shape[1]
    hd = d // MEM_HEADS
    h = _rms(x, gpre_ref[...]).astype(BF16)
    q = (_dot(h, wq_ref[...]) * (hd ** -0.5)).astype(BF16)
    outs = []
    for n in range(MEM_HEADS):
        cs = slice(n * hd, (n + 1) * hd)
        s = _dot_nt(q[:, cs], k_ref[:, cs])
        e = jnp.exp(s - jnp.max(s, axis=-1, keepdims=True))
        p = e / jnp.sum(e, axis=-1, keepdims=True)
        outs.append(_dot(p.astype(BF16), v_ref[:, cs]).astype(BF16))
    y = _dot(jnp.concatenate(outs, axis=1), wo_ref[...])
    o_ref[...] = x + _rms(y, gpost_ref[...])


def _memattn(x, l, gpre, wq, k, v, wo, gpost, batch, seq):
    t, d = x.shape
    mlen = k.shape[1]
    tm = MEM_TM
    nt = seq // tm
    row = pl.BlockSpec((tm, d), lambda b, i: (b * nt + i, 0))
    kvb = pl.BlockSpec((None, mlen, d), lambda b, i: (b, 0, 0))
    cs = functools.partial(_const_spec, layer=l)
    return pl.pallas_call(
        _memattn_kernel,
        out_shape=jax.ShapeDtypeStruct((t, d), F32),
        grid=(batch, nt),
        in_specs=[row, cs((1, d)), cs((d, d)), kvb, kvb, cs((d, d)), cs((1, d))],
        out_specs=row,
        compiler_params=_params("parallel", "arbitrary"),
        name="memattn",
    )(x, gpre, wq, k, v, wo, gpost)


def _overlap_matrix_t(n_cmp_pad, n_sel):
    cs = np.arange(n_cmp_pad)[None, :] * CMP_STRIDE
    ss = np.arange(LANES)[:, None] * SEL_BLOCK
    ov = np.minimum(cs + CMP_BLOCK, ss + SEL_BLOCK) - np.maximum(cs, ss)
    ov = np.maximum(ov, 0) / CMP_BLOCK
    ov[n_sel:, :] = 0.0
    return jnp.asarray(ov, dtype=BF16)


def _block_mask_matrix(seq):
    ex = (np.arange(seq)[:, None] // SEL_BLOCK) == np.arange(LANES)[None, :]
    return jnp.asarray(np.where(ex, -MASK_BIG, 0.0), dtype=BF16)


def kernel(x, mem, ff1_pre_g, ff1_w1, ff1_w3, ff1_w2, ff1_post_g, mix_pre_g, w_in, b_in, pool_w, pool_scale, cmp_pos_k, cmp_w1_k, cmp_w2_k, cmp_pos_v, cmp_w1_v, cmp_w2_v, gmlp_ln_g, gmlp_ln_b, gmlp_ws, gmlp_bs, w_br_pool, w_br_nsa, w_br_gmlp, w_mix_out, mix_post_g, mem_pre_g, mem_kv_g, mem_wq, mem_wk, mem_wv, mem_wo, mem_post_g, ff2_pre_g, ff2_w1, ff2_w3, ff2_w2, ff2_post_g):
    batch, seq, d = x.shape
    depth = w_in.shape[0]
    t = batch * seq
    n_cmp_rows = seq // CMP_STRIDE
    bf = lambda w: w.astype(BF16)
    vec = lambda g: g.reshape(depth, 1, -1)

    c_g = 1792
    c_gm = c_g + 3 * NSA_HEADS
    c_mg = c_gm + 2 * GMLP_WIDTH
    zeros_w = lambda n: jnp.zeros((depth, d, n), w_in.dtype)
    zeros_b = lambda n: jnp.zeros((depth, n), b_in.dtype)
    half = 3 * NSA_HPG
    w_proj = bf(jnp.concatenate(
        [w_in[:, :, :c_g], w_in[:, :, c_g:c_g + half], zeros_w(LANES - half),
         w_in[:, :, c_g + half:c_gm], zeros_w(LANES - half), w_in[:, :, c_gm:c_mg]], axis=2))
    b_proj = jnp.concatenate(
        [b_in[:, :c_g], b_in[:, c_g:c_g + half], zeros_b(LANES - half),
         b_in[:, c_g + half:c_gm], zeros_b(LANES - half), b_in[:, c_gm:c_mg]], axis=1).reshape(depth, 1, -1)
    w_mg = bf(w_in[:, :, c_mg:])
    b_mg = b_in[:, c_mg:].reshape(depth, 1, -1)
    bsx = jnp.broadcast_to(jnp.swapaxes(gmlp_bs, 1, 2)[:, :, :, None],
                           (depth, GMLP_CHUNK, GMLP_GROUPS, GMLP_WIDTH // GMLP_GROUPS)).reshape(
                               depth, GMLP_CHUNK, GMLP_WIDTH)
    half_w = CMP_STRIDE * NSA_HEAD_DIM
    cmp_pos = jnp.stack([cmp_pos_k, cmp_pos_v], axis=1).reshape(depth, 2, 2, half_w)
    cmp_w1 = bf(jnp.stack([cmp_w1_k, cmp_w1_v], axis=1)).reshape(depth, 2, 2, half_w, -1)
    cmp_w2 = bf(jnp.stack([cmp_w2_k, cmp_w2_v], axis=1))
    w_br_nsa_b = bf(w_br_nsa)
    ff1 = (vec(ff1_pre_g), bf(ff1_w1), bf(ff1_w3), bf(ff1_w2), vec(ff1_post_g))
    ff2 = (vec(ff2_pre_g), bf(ff2_w1), bf(ff2_w3), bf(ff2_w2), vec(ff2_post_g))
    pool_w_b, pool_scale_v = bf(pool_w), vec(pool_scale)
    w_br_pool_b, w_br_gmlp_b, w_mix_out_b = bf(w_br_pool), bf(w_br_gmlp), bf(w_mix_out)
    mem_wq_b, mem_wk_b, mem_wv_b, mem_wo_b = bf(mem_wq), bf(mem_wk), bf(mem_wv), bf(mem_wo)
    mix_pre, mix_post = vec(mix_pre_g), vec(mix_post_g)
    mem_pre, mem_kv, mem_post = vec(mem_pre_g), vec(mem_kv_g), vec(mem_post_g)
    ln_g, ln_b = vec(gmlp_ln_g), vec(gmlp_ln_b)
    ov = _overlap_matrix_t(n_cmp_rows, seq // SEL_BLOCK)
    ex = _block_mask_matrix(seq)

    xs = x.reshape(t, d)
    for l in range(depth):
        xs = _ffn(xs, l, *ff1)
        a, q_hm, kvc, kv4, gates, gm = _inproj(xs, l, mix_pre, w_proj, b_proj, gmlp_ws, ln_g, ln_b, bsx)
        pm = _pool(a, l, pool_w_b, pool_scale_v, batch, seq)
        a4 = kvc.reshape(batch, n_cmp_rows, CMP_STRIDE, 4, NSA_HEAD_DIM).transpose(0, 3, 1, 2, 4).reshape(
            batch, 4, n_cmp_rows, half_w)
        cmp = _cmpkv(a4, l, cmp_pos, cmp_w1, cmp_w2)
        o_nsa = _nsa(q_hm, cmp, kv4, gates, ov, ex, batch, seq)
        xs = _merge(xs, l, pm, o_nsa, gm, mix_pre, w_mg, b_mg, w_br_pool_b, w_br_nsa_b, w_br_gmlp_b,
                    w_mix_out_b, mix_post)
        mk, mv = _memkv(mem, l, mem_kv, mem_wk_b, mem_wv_b)
        xs = _memattn(xs, l, mem_pre, mem_wq_b, mk, mv, mem_wo_b, mem_post, batch, seq)
        xs = _ffn(xs, l, *ff2)
    return xs.reshape(batch, seq, d)
```

```python
import functools

import numpy as np
import jax
import jax.numpy as jnp
from jax import lax
from jax.experimental import pallas as pl
from jax.experimental.pallas import tpu as pltpu

F32 = jnp.float32
BF16 = jnp.bfloat16

EPS = 1e-6
NEG = -1e30
MASK_BIG = 2.0 ** 100
LOG2E = float(np.log2(np.e))

MEM_HEADS = 4
POOL_WINDOWS = (2, 4, 8, 16)
POOL_GROUP = 128
NSA_HEADS = 8
NSA_KV_HEADS = 2
NSA_HPG = NSA_HEADS // NSA_KV_HEADS
NSA_HEAD_DIM = 64
CMP_BLOCK = 32
CMP_STRIDE = 16
SEL_BLOCK = 64
SEL_TOPK = 8
WINDOW = 256
GMLP_WIDTH = 512
GMLP_GROUPS = 4
GMLP_CHUNK = 128
POOL_WIDTH = 512
NSA_WIDTH = NSA_HEADS * NSA_HEAD_DIM

LANES = 128
VMEM_LIMIT_BYTES = 56 * 1024 * 1024

FFN_TM = 512
FFN_CHUNK = 512
PROJ_TM = 512
MERGE_TM = 512
MEM_TM = 512
NSA_TQ = 512


def _dot(a, b):
    return jnp.dot(a, b, preferred_element_type=F32)


def _dot_nt(a, b):
    return lax.dot_general(a, b, (((1,), (1,)), ((), ())), preferred_element_type=F32)


def _rms(x, g):
    return x * lax.rsqrt(jnp.mean(x * x, axis=-1, keepdims=True) + EPS) * g


def _gelu(x):
    c = np.float32(np.sqrt(2.0 / np.pi))
    return x * (0.5 * (1.0 + jnp.tanh(c * (x + 0.044715 * (x * x * x)))))


def _const_spec(shape, layer=None):
    n = len(shape)
    if layer is None:
        return pl.BlockSpec(shape, lambda *_: (0,) * n, pipeline_mode=pl.Buffered(1))
    return pl.BlockSpec((None,) + tuple(shape), lambda *_: (layer,) + (0,) * n, pipeline_mode=pl.Buffered(1))


def _params(*sem):
    return pltpu.CompilerParams(dimension_semantics=sem, vmem_limit_bytes=VMEM_LIMIT_BYTES)


def _ffn_kernel(x_ref, gpre_ref, w1_ref, w3_ref, w2_ref, gpost_ref, o_ref, h_ref, acc_ref, *, chunks):
    x = x_ref[...]
    h_ref[...] = _rms(x, gpre_ref[...]).astype(BF16)
    for n, (off, width) in enumerate(chunks):
        h = h_ref[...]
        a = _dot(h, w1_ref[:, off:off + width])
        b = _dot(h, w3_ref[:, off:off + width])
        u = (a * jax.nn.sigmoid(a) * b).astype(BF16)
        y = _dot(u, w2_ref[off:off + width, :])
        if n == 0:
            acc_ref[...] = y
        else:
            acc_ref[...] += y
    o_ref[...] = x + 0.5 * _rms(acc_ref[...], gpost_ref[...])


def _ffn(x, l, gpre, w1, w3, w2, gpost):
    t, d = x.shape
    f = w1.shape[-1]
    tm = FFN_TM
    cs = functools.partial(_const_spec, layer=l)
    chunks = tuple((off, min(FFN_CHUNK, f - off)) for off in range(0, f, FFN_CHUNK))
    row = pl.BlockSpec((tm, d), lambda i: (i, 0))
    return pl.pallas_call(
        functools.partial(_ffn_kernel, chunks=chunks),
        out_shape=jax.ShapeDtypeStruct((t, d), F32),
        grid=(t // tm,),
        in_specs=[row, cs((1, d)), cs((d, f)), cs((d, f)), cs((f, d)), cs((1, d))],
        out_specs=row,
        scratch_shapes=[pltpu.VMEM((tm, d), BF16), pltpu.VMEM((tm, d), F32)],
        compiler_params=_params("parallel"),
        name="ffn",
    )(x, gpre, w1, w3, w2, gpost)


_A0, _Q0, _KV0, _G0, _GM0, _PROJ_COLS = 0, 512, 1024, 1792, 2048, 3072


def _inproj_kernel(x_ref, g_ref, w_ref, b_ref, ws_ref, lng_ref, lnb_ref, bsx_ref,
                   a_ref, q_ref, kvc_ref, kv4_ref, gate_ref, gm_ref, *, tm):
    h = _rms(x_ref[...], g_ref[...]).astype(BF16)
    z = _dot(h, w_ref[...]) + b_ref[...]
    a_ref[...] = z[:, _A0:_A0 + POOL_WIDTH]
    zq = z[:, _Q0:_Q0 + NSA_WIDTH] * (NSA_HEAD_DIM ** -0.5 * LOG2E)
    for hh in range(NSA_HEADS):
        q_ref[hh] = zq[:, hh * 64:(hh + 1) * 64].astype(BF16)
    kvc_ref[...] = z[:, _KV0:_KV0 + 256]
    for r in range(8):
        c0 = _KV0 + 256 + r * 64
        kv4_ref[r] = z[:, c0:c0 + 64].astype(BF16)
    gate_ref[...] = jax.nn.sigmoid(z[:, _G0:_G0 + 256])
    gm = _gelu(z[:, _GM0:_GM0 + 2 * GMLP_WIDTH])
    u = gm[:, :GMLP_WIDTH]
    v = gm[:, GMLP_WIDTH:]
    mu = jnp.mean(v, axis=-1, keepdims=True)
    vc = v - mu
    var = jnp.mean(vc * vc, axis=-1, keepdims=True)
    vn = (vc * lax.rsqrt(var + EPS) * lng_ref[...] + lnb_ref[...]).astype(BF16)
    ri = lax.broadcasted_iota(jnp.int32, (GMLP_CHUNK, GMLP_CHUNK), 0)
    ci = lax.broadcasted_iota(jnp.int32, (GMLP_CHUNK, GMLP_CHUNK), 1)
    for g in range(GMLP_GROUPS):
        wg = jnp.where(ri >= ci, ws_ref[g], 0.0).astype(BF16)
        cs = slice(g * 128, (g + 1) * 128)
        for c in range(tm // GMLP_CHUNK):
            rs = slice(c * GMLP_CHUNK, (c + 1) * GMLP_CHUNK)
            s = _dot(wg, vn[rs, cs]) + bsx_ref[:, cs]
            gm_ref[rs, cs] = (u[rs, cs] * s).astype(BF16)


def _inproj(x, l, g, w, b, ws, lng, lnb, bsx):
    t, d = x.shape
    tm = PROJ_TM
    cs = functools.partial(_const_spec, layer=l)
    row = lambda width: pl.BlockSpec((tm, width), lambda i: (i, 0))
    heads = pl.BlockSpec((8, tm, 64), lambda i: (0, i, 0))
    return pl.pallas_call(
        functools.partial(_inproj_kernel, tm=tm),
        out_shape=(
            jax.ShapeDtypeStruct((t, POOL_WIDTH), F32),
            jax.ShapeDtypeStruct((8, t, 64), BF16),
            jax.ShapeDtypeStruct((t, 256), F32),
            jax.ShapeDtypeStruct((8, t, 64), BF16),
            jax.ShapeDtypeStruct((t, 256), F32),
            jax.ShapeDtypeStruct((t, GMLP_WIDTH), BF16),
        ),
        grid=(t // tm,),
        in_specs=[row(d), cs((1, d)), cs((d, _PROJ_COLS)), cs((1, _PROJ_COLS)),
                  cs((GMLP_GROUPS, GMLP_CHUNK, GMLP_CHUNK)), cs((1, GMLP_WIDTH)), cs((1, GMLP_WIDTH)),
                  cs((GMLP_CHUNK, GMLP_WIDTH))],
        out_specs=(row(POOL_WIDTH), heads, row(256), heads, row(256), row(GMLP_WIDTH)),
        compiler_params=_params("parallel"),
        name="inproj",
    )(x, g, w, b, ws, lng, lnb, bsx)


_POOL_PAD = 16


def _pool_kernel(a_ref, pw_ref, ps_ref, o_ref, pad_ref, *, seq):
    pos = lax.broadcasted_iota(jnp.int32, (seq, 1), 0).astype(F32) + 1.0
    pad_ref[0:_POOL_PAD, :] = jnp.zeros((_POOL_PAD, POOL_GROUP), F32)
    for gi, w in enumerate(POOL_WINDOWS):
        cs = slice(gi * POOL_GROUP, (gi + 1) * POOL_GROUP)
        x = a_ref[:, cs]
        s = x
        k = 1
        while k < w:
            pad_ref[_POOL_PAD:_POOL_PAD + seq, :] = s
            s = s + pad_ref[_POOL_PAD - k:_POOL_PAD - k + seq, :]
            k *= 2
        pooled = (s / jnp.minimum(pos, float(w)) - x).astype(BF16)
        o_ref[:, cs] = (_dot(pooled, pw_ref[gi]) * ps_ref[:, cs]).astype(BF16)


def _pool(a, l, pw, ps, batch, seq):
    t = a.shape[0]
    blk = pl.BlockSpec((seq, POOL_WIDTH), lambda b: (b, 0))
    cs = functools.partial(_const_spec, layer=l)
    return pl.pallas_call(
        functools.partial(_pool_kernel, seq=seq),
        out_shape=jax.ShapeDtypeStruct((t, POOL_WIDTH), BF16),
        grid=(batch,),
        in_specs=[blk, cs((len(POOL_WINDOWS), POOL_GROUP, POOL_GROUP)), cs((1, POOL_WIDTH))],
        out_specs=blk,
        scratch_shapes=[pltpu.VMEM((_POOL_PAD + seq, POOL_GROUP), F32)],
        compiler_params=_params("parallel"),
        name="pool",
    )(a, pw, ps)


def _cmpkv_kernel(a_ref, pos_ref, w1_ref, w2_ref, o_ref):
    a = a_ref[...]
    xa = (a + pos_ref[0:1, :]).astype(BF16)
    xb = (a + pos_ref[1:2, :]).astype(BF16)
    p = _dot(xa, w1_ref[0])
    q = _dot(xb, w1_ref[1])
    nrow = q.shape[0]
    hpre = p + pltpu.roll(q, nrow - 1, axis=0)
    o_ref[...] = _dot(_gelu(hpre).astype(BF16), w2_ref[...]).astype(BF16)


def _cmpkv(a4, l, pos, w1, w2):
    batch, _, nrow, width = a4.shape
    hid = w1.shape[-1]
    return pl.pallas_call(
        _cmpkv_kernel,
        out_shape=jax.ShapeDtypeStruct((batch, 4, nrow, NSA_HEAD_DIM), BF16),
        grid=(batch, 4),
        in_specs=[pl.BlockSpec((None, None, nrow, width), lambda b, j: (b, j, 0, 0)),
                  pl.BlockSpec((None, None, 2, width), lambda b, j: (l, j // 2, 0, 0)),
                  pl.BlockSpec((None, None, 2, width, hid), lambda b, j: (l, j // 2, 0, 0, 0)),
                  pl.BlockSpec((None, None, hid, NSA_HEAD_DIM), lambda b, j: (l, j // 2, 0, 0))],
        out_specs=pl.BlockSpec((None, None, nrow, NSA_HEAD_DIM), lambda b, j: (b, j, 0, 0)),
        compiler_params=_params("parallel", "arbitrary"),
        name="cmpkv",
    )(a4, pos, w1, w2)


def _transpose_bf16(eye, x):
    return _dot_nt(eye, x).astype(BF16)


def _nsa_kernel(q_ref, cmp_ref, kv_ref, gate_ref, ovt_ref, extn_ref, o_ref, kaug_ref, vst_ref, vwt_ref, *, tq):
    i = pl.program_id(1)
    tk = tq
    hd = NSA_HEAD_DIM
    m4 = NSA_HPG * tq
    seq = kv_ref.shape[1]
    n_sel = seq // SEL_BLOCK
    groups = range(NSA_KV_HEADS)
    eye = (lax.broadcasted_iota(jnp.int32, (hd, hd), 0)
           == lax.broadcasted_iota(jnp.int32, (hd, hd), 1)).astype(BF16)

    @pl.when(i == 0)
    def _():
        place = (lax.broadcasted_iota(jnp.int32, (hd, LANES), 0)
                 == lax.broadcasted_iota(jnp.int32, (hd, LANES), 1)).astype(BF16)
        for g in groups:
            kaug_ref[g, :, 0:LANES] = extn_ref[...]
            for kt in range(seq // tk):
                rs = slice(kt * tk, (kt + 1) * tk)
                kaug_ref[g, rs, LANES:2 * LANES] = _dot(kv_ref[g, rs, :], place).astype(BF16)
                vst_ref[g, :, rs] = _transpose_bf16(eye, kv_ref[2 + g, rs, :])
                vwt_ref[g, :, rs] = _transpose_bf16(eye, kv_ref[6 + g, rs, :])

    col = lax.broadcasted_iota(jnp.int32, (1, m4), 1)
    t4 = i * tq + jnp.bitwise_and(col, tq - 1)
    t1 = i * tq + lax.broadcasted_iota(jnp.int32, (1, tq), 1)
    krow = lax.broadcasted_iota(jnp.int32, (tk, 1), 0)
    m0 = jnp.full((1, m4), NEG, F32)
    l0 = jnp.zeros((1, m4), F32)
    acc0 = jnp.zeros((hd, m4), F32)

    def attend(s, vt_tile, m_run, l_run, acc):
        m_new = jnp.maximum(m_run, jnp.max(s, axis=0, keepdims=True))
        alpha = jnp.exp2(m_run - m_new)
        e = jnp.exp2(s - m_new)
        l_new = alpha * l_run + jnp.sum(e, axis=0, keepdims=True)
        return m_new, l_new, alpha * acc + _dot(vt_tile, e.astype(BF16))

    qt = [jnp.concatenate([_transpose_bf16(eye, q_ref[NSA_HPG * g + h]) for h in range(NSA_HPG)], axis=1)
          for g in groups]

    n_w = WINDOW + tq
    off_w = pl.multiple_of(jnp.maximum(i * tq - WINDOW, 0), min(tq, WINDOW))
    kp_w = off_w + lax.broadcasted_iota(jnp.int32, (n_w, 1), 0)
    bias_w = jnp.where((kp_w > t1 - WINDOW) & (kp_w <= t1), 0.0, NEG)
    bias_w4 = jnp.concatenate([bias_w] * NSA_HPG, axis=1)

    def window(g):
        s = _dot(kv_ref[4 + g, pl.ds(off_w, n_w), :], qt[g]) + bias_w4
        e = jnp.exp2(s - jnp.max(s, axis=0, keepdims=True))
        acc = _dot(vwt_ref[g, :, pl.ds(off_w, n_w)], e.astype(BF16))
        return acc / jnp.sum(e, axis=0, keepdims=True)

    def compressed(g):
        ncmp = cmp_ref.shape[1]
        crow = lax.broadcasted_iota(jnp.int32, (ncmp, 1), 0)
        valid = (crow * CMP_STRIDE + (CMP_BLOCK - 1)) <= t4
        s = jnp.where(valid, _dot(cmp_ref[g], qt[g]), NEG)
        e = jnp.exp2(s - jnp.max(s, axis=0, keepdims=True))
        p = jnp.where(valid, e / jnp.sum(e, axis=0, keepdims=True), 0.0)
        o_cmp = _dot(_transpose_bf16(eye, cmp_ref[2 + g]), p.astype(BF16))
        psum = p[:, 0:tq] + p[:, tq:2 * tq] + p[:, 2 * tq:3 * tq] + p[:, 3 * tq:4 * tq]
        p_hi = psum.astype(BF16)
        p_lo = (psum - p_hi.astype(F32)).astype(BF16)
        imp = (_dot(ovt_ref[...], p_hi) + _dot(ovt_ref[...], p_lo))[0:n_sel, :]
        jrow = lax.broadcasted_iota(jnp.int32, (n_sel, 1), 0)
        cur = jnp.right_shift(t1, 6)
        forced = (jrow == 0) | (jrow == cur) | (jrow == cur - 1)
        vt = jnp.where(forced, 1e9, jnp.where(jrow * SEL_BLOCK <= t1, imp, -1.0))
        n_acc = 4
        cnt = [jnp.zeros((n_sel, tq), F32) for _ in range(n_acc)]
        for jp in range(n_sel):
            r = vt[jp:jp + 1, :]
            tie = jnp.where(jrow > jp, 1.0, 0.0)
            cnt[jp % n_acc] = cnt[jp % n_acc] + jnp.where(r > vt, 1.0, jnp.where(r == vt, tie, 0.0))
        rank = (cnt[0] + cnt[1]) + (cnt[2] + cnt[3])
        notsel = jnp.where(rank < float(SEL_TOPK), 0.0, 1.0)
        notsel = jnp.concatenate([notsel, jnp.zeros((LANES - n_sel, tq), F32)], axis=0).astype(BF16)
        qaug = jnp.concatenate([jnp.concatenate([notsel] * NSA_HPG, axis=1), qt[g], jnp.zeros((hd, m4), BF16)],
                               axis=0)
        return o_cmp, qaug

    o_win = [window(g) for g in groups]
    o_cmp, qaug = zip(*[compressed(g) for g in groups])

    def scores(g, kt):
        off = pl.multiple_of(kt * tk, tk)
        return _dot(kaug_ref[g, pl.ds(off, tk), :], qaug[g])

    def slc_body(kt, carry):
        off = pl.multiple_of(kt * tk, tk)
        return tuple(attend(scores(g, kt), vst_ref[g, :, pl.ds(off, tk)], *carry[g]) for g in groups)

    carry = lax.fori_loop(0, i, slc_body, tuple((m0, l0, acc0) for g in groups))
    causal = jnp.where((i * tk + krow) <= t1, 0.0, NEG)
    causal4 = jnp.concatenate([causal] * NSA_HPG, axis=1)
    off_d = pl.multiple_of(i * tk, tk)
    o_slc = []
    for g in groups:
        _, l_fin, acc_fin = attend(scores(g, i) + causal4, vst_ref[g, :, pl.ds(off_d, tk)], *carry[g])
        o_slc.append(acc_fin / l_fin)

    eye_q = (lax.broadcasted_iota(jnp.int32, (tq, tq), 0)
             == lax.broadcasted_iota(jnp.int32, (tq, tq), 1)).astype(BF16)
    o_heads = []
    for g in groups:
        gt = gate_ref[:, g * LANES:(g + 1) * LANES].T
        for h in range(NSA_HPG):
            cs = slice(h * tq, (h + 1) * tq)
            o = (gt[3 * h:3 * h + 1, :] * o_cmp[g][:, cs] + gt[3 * h + 1:3 * h + 2, :] * o_slc[g][:, cs]
                 + gt[3 * h + 2:3 * h + 3, :] * o_win[g][:, cs])
            o_heads.append(o.astype(BF16))
    o_ref[...] = _transpose_bf16(eye_q, jnp.concatenate(o_heads, axis=0))


def _nsa(q_hm, cmp, kv4, gates, ovt, extn, batch, seq):
    t = q_hm.shape[1]
    tq = NSA_TQ
    assert (WINDOW % tq == 0 or tq % WINDOW == 0) and WINDOW + tq <= seq
    nq = seq // tq
    ncmp = cmp.shape[2]
    qblk = pl.BlockSpec((NSA_HEADS, tq, NSA_HEAD_DIM), lambda b, i: (0, b * nq + i, 0))
    return pl.pallas_call(
        functools.partial(_nsa_kernel, tq=tq),
        out_shape=jax.ShapeDtypeStruct((t, NSA_WIDTH), BF16),
        grid=(batch, nq),
        in_specs=[qblk,
                  pl.BlockSpec((None, 4, ncmp, NSA_HEAD_DIM), lambda b, i: (b, 0, 0, 0)),
                  pl.BlockSpec((8, seq, NSA_HEAD_DIM), lambda b, i: (0, b, 0)),
                  pl.BlockSpec((tq, 2 * LANES), lambda b, i: (b * nq + i, 0)),
                  _const_spec(ovt.shape), _const_spec(extn.shape)],
        out_specs=pl.BlockSpec((tq, NSA_WIDTH), lambda b, i: (b * nq + i, 0)),
        scratch_shapes=[pltpu.VMEM((NSA_KV_HEADS, seq, 2 * LANES), BF16),
                        pltpu.VMEM((NSA_KV_HEADS, NSA_HEAD_DIM, seq), BF16),
                        pltpu.VMEM((NSA_KV_HEADS, NSA_HEAD_DIM, seq), BF16)],
        compiler_params=_params("arbitrary", "arbitrary"),
        name="nsa",
    )(q_hm, cmp, kv4, gates, ovt, extn)


def _merge_kernel(x_ref, pm_ref, on_ref, gm_ref, gpre_ref, wmg_ref, bmg_ref, wbp_ref, wbn_ref, wbg_ref,
                  wout_ref, gpost_ref, o_ref):
    x = x_ref[...]
    d = x.shape[1]
    h = _rms(x, gpre_ref[...]).astype(BF16)

    def gate(j):
        return jax.nn.sigmoid(_dot(h, wmg_ref[:, j * d:(j + 1) * d]) + bmg_ref[:, j * d:(j + 1) * d])

    comb = gate(0) * _dot(pm_ref[...], wbp_ref[...])
    comb = comb + gate(1) * _dot(on_ref[...], wbn_ref[...])
    comb = comb + gate(2) * _dot(gm_ref[...], wbg_ref[...])
    y = _dot(comb.astype(BF16), wout_ref[...])
    o_ref[...] = x + _rms(y, gpost_ref[...])


def _merge(x, l, pm, o_nsa, gm, gpre, wmg, bmg, wbp, wbn, wbg, wout, gpost):
    t, d = x.shape
    tm = MERGE_TM
    row = lambda width: pl.BlockSpec((tm, width), lambda i: (i, 0))
    cs = functools.partial(_const_spec, layer=l)
    return pl.pallas_call(
        _merge_kernel,
        out_shape=jax.ShapeDtypeStruct((t, d), F32),
        grid=(t // tm,),
        in_specs=[row(d), row(POOL_WIDTH), row(NSA_WIDTH), row(GMLP_WIDTH), cs((1, d)), cs((d, 3 * d)),
                  cs((1, 3 * d)), cs((POOL_WIDTH, d)), cs((NSA_WIDTH, d)), cs((GMLP_WIDTH, d)), cs((d, d)),
                  cs((1, d))],
        out_specs=row(d),
        compiler_params=_params("parallel"),
        name="merge",
    )(x, pm, o_nsa, gm, gpre, wmg, bmg, wbp, wbn, wbg, wout, gpost)


def _memkv_kernel(m_ref, g_ref, wk_ref, wv_ref, k_ref, v_ref):
    mn = _rms(m_ref[...], g_ref[...]).astype(BF16)
    k_ref[...] = _dot(mn, wk_ref[...]).astype(BF16)
    v_ref[...] = _dot(mn, wv_ref[...]).astype(BF16)


def _memkv(mem, l, g, wk, wv):
    batch, mlen, d = mem.shape
    blk = pl.BlockSpec((None, mlen, d), lambda b: (b, 0, 0))
    out = jax.ShapeDtypeStruct((batch, mlen, d), BF16)
    cs = functools.partial(_const_spec, layer=l)
    return pl.pallas_call(
        _memkv_kernel,
        out_shape=(out, out),
        grid=(batch,),
        in_specs=[blk, cs((1, d)), cs((d, d)), cs((d, d))],
        out_specs=(blk, blk),
        compiler_params=_params("parallel"),
        name="memkv",
    )(mem, g, wk, wv)


def _memattn_kernel(x_ref, gpre_ref, wq_ref, k_ref, v_ref, wo_ref, gpost_ref, o_ref):
    x = x_ref[...]
    d = x.shape[1]
    hd = d // MEM_HEADS
    h = _rms(x, gpre_ref[...]).astype(BF16)
    q = (_dot(h, wq_ref[...]) * (hd ** -0.5)).astype(BF16)
    outs = []
    for n in range(MEM_HEADS):
        cs = slice(n * hd, (n + 1) * hd)
        s = _dot_nt(q[:, cs], k_ref[:, cs])
        e = jnp.exp(s - jnp.max(s, axis=-1, keepdims=True))
        p = e / jnp.sum(e, axis=-1, keepdims=True)
        outs.append(_dot(p.astype(BF16), v_ref[:, cs]).astype(BF16))
    y = _dot(jnp.concatenate(outs, axis=1), wo_ref[...])
    o_ref[...] = x + _rms(y, gpost_ref[...])


def _memattn(x, l, gpre, wq, k, v, wo, gpost, batch, seq):
    t, d = x.shape
    mlen = k.shape[1]
    tm = MEM_TM
    nt = seq // tm
    row = pl.BlockSpec((tm, d), lambda b, i: (b * nt + i, 0))
    kvb = pl.BlockSpec((None, mlen, d), lambda b, i: (b, 0, 0))
    cs = functools.partial(_const_spec, layer=l)
    return pl.pallas_call(
        _memattn_kernel,
        out_shape=jax.ShapeDtypeStruct((t, d), F32),
        grid=(batch, nt),
        in_specs=[row, cs((1, d)), cs((d, d)), kvb, kvb, cs((d, d)), cs((1, d))],
        out_specs=row,
        compiler_params=_params("parallel", "arbitrary"),
        name="memattn",
    )(x, gpre, wq, k, v, wo, gpost)


def _overlap_matrix_t(n_cmp_pad, n_sel):
    cs = np.arange(n_cmp_pad)[None, :] * CMP_STRIDE
    ss = np.arange(LANES)[:, None] * SEL_BLOCK
    ov = np.minimum(cs + CMP_BLOCK, ss + SEL_BLOCK) - np.maximum(cs, ss)
    ov = np.maximum(ov, 0) / CMP_BLOCK
    ov[n_sel:, :] = 0.0
    return jnp.asarray(ov, dtype=BF16)


def _block_mask_matrix(seq):
    ex = (np.arange(seq)[:, None] // SEL_BLOCK) == np.arange(LANES)[None, :]
    return jnp.asarray(np.where(ex, -MASK_BIG, 0.0), dtype=BF16)


def kernel(x, mem, ff1_pre_g, ff1_w1, ff1_w3, ff1_w2, ff1_post_g, mix_pre_g, w_in, b_in, pool_w, pool_scale, cmp_pos_k, cmp_w1_k, cmp_w2_k, cmp_pos_v, cmp_w1_v, cmp_w2_v, gmlp_ln_g, gmlp_ln_b, gmlp_ws, gmlp_bs, w_br_pool, w_br_nsa, w_br_gmlp, w_mix_out, mix_post_g, mem_pre_g, mem_kv_g, mem_wq, mem_wk, mem_wv, mem_wo, mem_post_g, ff2_pre_g, ff2_w1, ff2_w3, ff2_w2, ff2_post_g):
    batch, seq, d = x.shape
    depth = w_in.shape[0]
    t = batch * seq
    n_cmp_rows = seq // CMP_STRIDE
    bf = lambda w: w.astype(BF16)
    vec = lambda g: g.reshape(depth, 1, -1)

    c_g = 1792
    c_gm = c_g + 3 * NSA_HEADS
    c_mg = c_gm + 2 * GMLP_WIDTH
    w_in_b = bf(w_in)
    zeros_w = lambda n: jnp.zeros((depth, d, n), BF16)
    zeros_b = lambda n: jnp.zeros((depth, n), b_in.dtype)
    half = 3 * NSA_HPG
    w_proj = jnp.concatenate(
        [w_in_b[:, :, :c_g], w_in_b[:, :, c_g:c_g + half], zeros_w(LANES - half),
         w_in_b[:, :, c_g + half:c_gm], zeros_w(LANES - half), w_in_b[:, :, c_gm:c_mg]], axis=2)
    b_proj = jnp.concatenate(
        [b_in[:, :c_g], b_in[:, c_g:c_g + half], zeros_b(LANES - half),
         b_in[:, c_g + half:c_gm], zeros_b(LANES - half), b_in[:, c_gm:c_mg]], axis=1).reshape(depth, 1, -1)
    w_mg = w_in_b[:, :, c_mg:]
    b_mg = b_in[:, c_mg:].reshape(depth, 1, -1)
    bsx = jnp.broadcast_to(jnp.swapaxes(gmlp_bs, 1, 2)[:, :, :, None],
                           (depth, GMLP_CHUNK, GMLP_GROUPS, GMLP_WIDTH // GMLP_GROUPS)).reshape(
                               depth, GMLP_CHUNK, GMLP_WIDTH)
    half_w = CMP_STRIDE * NSA_HEAD_DIM
    cmp_pos = jnp.stack([cmp_pos_k, cmp_pos_v], axis=1).reshape(depth, 2, 2, half_w)
    cmp_w1 = bf(jnp.stack([cmp_w1_k, cmp_w1_v], axis=1)).reshape(depth, 2, 2, half_w, -1)
    cmp_w2 = bf(jnp.stack([cmp_w2_k, cmp_w2_v], axis=1))
    w_br_nsa_b = bf(w_br_nsa)
    ff1 = (vec(ff1_pre_g), bf(ff1_w1), bf(ff1_w3), bf(ff1_w2), vec(ff1_post_g))
    ff2 = (vec(ff2_pre_g), bf(ff2_w1), bf(ff2_w3), bf(ff2_w2), vec(ff2_post_g))
    pool_w_b, pool_scale_v = bf(pool_w), vec(pool_scale)
    w_br_pool_b, w_br_gmlp_b, w_mix_out_b = bf(w_br_pool), bf(w_br_gmlp), bf(w_mix_out)
    mem_wq_b, mem_wk_b, mem_wv_b, mem_wo_b = bf(mem_wq), bf(mem_wk), bf(mem_wv), bf(mem_wo)
    mix_pre, mix_post = vec(mix_pre_g), vec(mix_post_g)
    mem_pre, mem_kv, mem_post = vec(mem_pre_g), vec(mem_kv_g), vec(mem_post_g)
    ln_g, ln_b = vec(gmlp_ln_g), vec(gmlp_ln_b)
    ov = _overlap_matrix_t(n_cmp_rows, seq // SEL_BLOCK)
    ex = _block_mask_matrix(seq)

    xs = x.reshape(t, d)
    for l in range(depth):
        xs = _ffn(xs, l, *ff1)
        a, q_hm, kvc, kv4, gates, gm = _inproj(xs, l, mix_pre, w_proj, b_proj, gmlp_ws, ln_g, ln_b, bsx)
        pm = _pool(a, l, pool_w_b, pool_scale_v, batch, seq)
        a4 = kvc.reshape(batch, n_cmp_rows, CMP_STRIDE, 4, NSA_HEAD_DIM).transpose(0, 3, 1, 2, 4).reshape(
            batch, 4, n_cmp_rows, half_w)
        cmp = _cmpkv(a4, l, cmp_pos, cmp_w1, cmp_w2)
        o_nsa = _nsa(q_hm, cmp, kv4, gates, ov, ex, batch, seq)
        xs = _merge(xs, l, pm, o_nsa, gm, mix_pre, w_mg, b_mg, w_br_pool_b, w_br_nsa_b, w_br_gmlp_b,
                    w_mix_out_b, mix_post)
        mk, mv = _memkv(mem, l, mem_kv, mem_wk_b, mem_wv_b)
        xs = _memattn(xs, l, mem_pre, mem_wq_b, mk, mv, mem_wo_b, mem_post, batch, seq)
        xs = _ffn(xs, l, *ff2)
    return xs.reshape(batch, seq, d)
```

```python
import functools

import numpy as np
import jax
import jax.numpy as jnp
from jax import lax
from jax.experimental import pallas as pl
from jax.experimental.pallas import tpu as pltpu

F32 = jnp.float32
BF16 = jnp.bfloat16

EPS = 1e-6
NEG = -1e30
MASK_BIG = 2.0 ** 100
LOG2E = float(np.log2(np.e))

MEM_HEADS = 4
POOL_WINDOWS = (2, 4, 8, 16)
POOL_GROUP = 128
NSA_HEADS = 8
NSA_KV_HEADS = 2
NSA_HPG = NSA_HEADS // NSA_KV_HEADS
NSA_HEAD_DIM = 64
CMP_BLOCK = 32
CMP_STRIDE = 16
SEL_BLOCK = 64
SEL_TOPK = 8
WINDOW = 256
GMLP_WIDTH = 512
GMLP_GROUPS = 4
GMLP_CHUNK = 128
POOL_WIDTH = 512
NSA_WIDTH = NSA_HEADS * NSA_HEAD_DIM

LANES = 128
VMEM_LIMIT_BYTES = 56 * 1024 * 1024

FFN_TM = 512
FFN_CHUNK = 512
PROJ_TM = 512
MERGE_TM = 512
MEM_TM = 512
NSA_TQ = 512
WIN_TQ = 128


def _dot(a, b):
    return jnp.dot(a, b, preferred_element_type=F32)


def _dot_nt(a, b):
    return lax.dot_general(a, b, (((1,), (1,)), ((), ())), preferred_element_type=F32)


def _rms(x, g):
    return x * lax.rsqrt(jnp.mean(x * x, axis=-1, keepdims=True) + EPS) * g


def _gelu(x):
    c = np.float32(np.sqrt(2.0 / np.pi))
    return x * (0.5 * (1.0 + jnp.tanh(c * (x + 0.044715 * (x * x * x)))))


def _const_spec(shape, layer=None):
    n = len(shape)
    if layer is None:
        return pl.BlockSpec(shape, lambda *_: (0,) * n, pipeline_mode=pl.Buffered(1))
    return pl.BlockSpec((None,) + tuple(shape), lambda *_: (layer,) + (0,) * n, pipeline_mode=pl.Buffered(1))


def _params(*sem):
    return pltpu.CompilerParams(dimension_semantics=sem, vmem_limit_bytes=VMEM_LIMIT_BYTES)


def _ffn_kernel(x_ref, gpre_ref, w1_ref, w3_ref, w2_ref, gpost_ref, o_ref, h_ref, acc_ref, *, chunks):
    x = x_ref[...]
    h_ref[...] = _rms(x, gpre_ref[...]).astype(BF16)
    for n, (off, width) in enumerate(chunks):
        h = h_ref[...]
        a = _dot(h, w1_ref[:, off:off + width])
        b = _dot(h, w3_ref[:, off:off + width])
        u = (a * jax.nn.sigmoid(a) * b).astype(BF16)
        y = _dot(u, w2_ref[off:off + width, :])
        if n == 0:
            acc_ref[...] = y
        else:
            acc_ref[...] += y
    o_ref[...] = x + 0.5 * _rms(acc_ref[...], gpost_ref[...])


def _ffn(x, l, gpre, w1, w3, w2, gpost):
    t, d = x.shape
    f = w1.shape[-1]
    tm = FFN_TM
    cs = functools.partial(_const_spec, layer=l)
    chunks = tuple((off, min(FFN_CHUNK, f - off)) for off in range(0, f, FFN_CHUNK))
    row = pl.BlockSpec((tm, d), lambda i: (i, 0))
    return pl.pallas_call(
        functools.partial(_ffn_kernel, chunks=chunks),
        out_shape=jax.ShapeDtypeStruct((t, d), F32),
        grid=(t // tm,),
        in_specs=[row, cs((1, d)), cs((d, f)), cs((d, f)), cs((f, d)), cs((1, d))],
        out_specs=row,
        scratch_shapes=[pltpu.VMEM((tm, d), BF16), pltpu.VMEM((tm, d), F32)],
        compiler_params=_params("parallel"),
        name="ffn",
    )(x, gpre, w1, w3, w2, gpost)


_A0, _Q0, _KV0, _G0, _GM0, _PROJ_COLS = 0, 512, 1024, 1792, 2048, 3072


def _inproj_kernel(x_ref, g_ref, w_ref, b_ref, ws_ref, lng_ref, lnb_ref, bsx_ref,
                   a_ref, q_ref, kvc_ref, kv4_ref, gate_ref, gm_ref, *, tm):
    h = _rms(x_ref[...], g_ref[...]).astype(BF16)
    z = _dot(h, w_ref[...]) + b_ref[...]
    a_ref[...] = z[:, _A0:_A0 + POOL_WIDTH]
    zq = z[:, _Q0:_Q0 + NSA_WIDTH] * (NSA_HEAD_DIM ** -0.5 * LOG2E)
    for hh in range(NSA_HEADS):
        q_ref[hh] = zq[:, hh * 64:(hh + 1) * 64].astype(BF16)
    kvc_ref[...] = z[:, _KV0:_KV0 + 256]
    for r in range(8):
        c0 = _KV0 + 256 + r * 64
        kv4_ref[r] = z[:, c0:c0 + 64].astype(BF16)
    gate_ref[...] = jax.nn.sigmoid(z[:, _G0:_G0 + 256])
    gm = _gelu(z[:, _GM0:_GM0 + 2 * GMLP_WIDTH])
    u = gm[:, :GMLP_WIDTH]
    v = gm[:, GMLP_WIDTH:]
    mu = jnp.mean(v, axis=-1, keepdims=True)
    vc = v - mu
    var = jnp.mean(vc * vc, axis=-1, keepdims=True)
    vn = (vc * lax.rsqrt(var + EPS) * lng_ref[...] + lnb_ref[...]).astype(BF16)
    ri = lax.broadcasted_iota(jnp.int32, (GMLP_CHUNK, GMLP_CHUNK), 0)
    ci = lax.broadcasted_iota(jnp.int32, (GMLP_CHUNK, GMLP_CHUNK), 1)
    for g in range(GMLP_GROUPS):
        wg = jnp.where(ri >= ci, ws_ref[g], 0.0).astype(BF16)
        cs = slice(g * 128, (g + 1) * 128)
        for c in range(tm // GMLP_CHUNK):
            rs = slice(c * GMLP_CHUNK, (c + 1) * GMLP_CHUNK)
            s = _dot(wg, vn[rs, cs]) + bsx_ref[:, cs]
            gm_ref[rs, cs] = (u[rs, cs] * s).astype(BF16)


def _inproj(x, l, g, w, b, ws, lng, lnb, bsx):
    t, d = x.shape
    tm = PROJ_TM
    cs = functools.partial(_const_spec, layer=l)
    row = lambda width: pl.BlockSpec((tm, width), lambda i: (i, 0))
    heads = pl.BlockSpec((8, tm, 64), lambda i: (0, i, 0))
    return pl.pallas_call(
        functools.partial(_inproj_kernel, tm=tm),
        out_shape=(
            jax.ShapeDtypeStruct((t, POOL_WIDTH), F32),
            jax.ShapeDtypeStruct((8, t, 64), BF16),
            jax.ShapeDtypeStruct((t, 256), F32),
            jax.ShapeDtypeStruct((8, t, 64), BF16),
            jax.ShapeDtypeStruct((t, 256), F32),
            jax.ShapeDtypeStruct((t, GMLP_WIDTH), BF16),
        ),
        grid=(t // tm,),
        in_specs=[row(d), cs((1, d)), cs((d, _PROJ_COLS)), cs((1, _PROJ_COLS)),
                  cs((GMLP_GROUPS, GMLP_CHUNK, GMLP_CHUNK)), cs((1, GMLP_WIDTH)), cs((1, GMLP_WIDTH)),
                  cs((GMLP_CHUNK, GMLP_WIDTH))],
        out_specs=(row(POOL_WIDTH), heads, row(256), heads, row(256), row(GMLP_WIDTH)),
        compiler_params=_params("parallel"),
        name="inproj",
    )(x, g, w, b, ws, lng, lnb, bsx)


_POOL_PAD = 16


def _pool_kernel(a_ref, pw_ref, ps_ref, o_ref, pad_ref, *, seq):
    pos = lax.broadcasted_iota(jnp.int32, (seq, 1), 0).astype(F32) + 1.0
    pad_ref[0:_POOL_PAD, :] = jnp.zeros((_POOL_PAD, POOL_GROUP), F32)
    for gi, w in enumerate(POOL_WINDOWS):
        cs = slice(gi * POOL_GROUP, (gi + 1) * POOL_GROUP)
        x = a_ref[:, cs]
        s = x
        k = 1
        while k < w:
            pad_ref[_POOL_PAD:_POOL_PAD + seq, :] = s
            s = s + pad_ref[_POOL_PAD - k:_POOL_PAD - k + seq, :]
            k *= 2
        pooled = (s / jnp.minimum(pos, float(w)) - x).astype(BF16)
        o_ref[:, cs] = (_dot(pooled, pw_ref[gi]) * ps_ref[:, cs]).astype(BF16)


def _pool(a, l, pw, ps, batch, seq):
    t = a.shape[0]
    blk = pl.BlockSpec((seq, POOL_WIDTH), lambda b: (b, 0))
    cs = functools.partial(_const_spec, layer=l)
    return pl.pallas_call(
        functools.partial(_pool_kernel, seq=seq),
        out_shape=jax.ShapeDtypeStruct((t, POOL_WIDTH), BF16),
        grid=(batch,),
        in_specs=[blk, cs((len(POOL_WINDOWS), POOL_GROUP, POOL_GROUP)), cs((1, POOL_WIDTH))],
        out_specs=blk,
        scratch_shapes=[pltpu.VMEM((_POOL_PAD + seq, POOL_GROUP), F32)],
        compiler_params=_params("parallel"),
        name="pool",
    )(a, pw, ps)


def _cmpkv_kernel(a_ref, pos_ref, w1_ref, w2_ref, o_ref):
    a = a_ref[...]
    xa = (a + pos_ref[0:1, :]).astype(BF16)
    xb = (a + pos_ref[1:2, :]).astype(BF16)
    p = _dot(xa, w1_ref[0])
    q = _dot(xb, w1_ref[1])
    nrow = q.shape[0]
    hpre = p + pltpu.roll(q, nrow - 1, axis=0)
    o_ref[...] = _dot(_gelu(hpre).astype(BF16), w2_ref[...]).astype(BF16)


def _cmpkv(a4, l, pos, w1, w2):
    batch, _, nrow, width = a4.shape
    hid = w1.shape[-1]
    return pl.pallas_call(
        _cmpkv_kernel,
        out_shape=jax.ShapeDtypeStruct((batch, 4, nrow, NSA_HEAD_DIM), BF16),
        grid=(batch, 4),
        in_specs=[pl.BlockSpec((None, None, nrow, width), lambda b, j: (b, j, 0, 0)),
                  pl.BlockSpec((None, None, 2, width), lambda b, j: (l, j // 2, 0, 0)),
                  pl.BlockSpec((None, None, 2, width, hid), lambda b, j: (l, j // 2, 0, 0, 0)),
                  pl.BlockSpec((None, None, hid, NSA_HEAD_DIM), lambda b, j: (l, j // 2, 0, 0))],
        out_specs=pl.BlockSpec((None, None, nrow, NSA_HEAD_DIM), lambda b, j: (b, j, 0, 0)),
        compiler_params=_params("parallel", "arbitrary"),
        name="cmpkv",
    )(a4, pos, w1, w2)


def _transpose_bf16(eye, x):
    return _dot_nt(eye, x).astype(BF16)


def _nsa_kernel(q_ref, cmp_ref, kv_ref, gate_ref, ovt_ref, extn_ref, o_ref, kaug_ref, vst_ref, vwt_ref, *, tq):
    i = pl.program_id(1)
    tk = tq
    hd = NSA_HEAD_DIM
    m4 = NSA_HPG * tq
    seq = kv_ref.shape[1]
    n_sel = seq // SEL_BLOCK
    groups = range(NSA_KV_HEADS)
    eye = (lax.broadcasted_iota(jnp.int32, (hd, hd), 0)
           == lax.broadcasted_iota(jnp.int32, (hd, hd), 1)).astype(BF16)

    @pl.when(i == 0)
    def _():
        place = (lax.broadcasted_iota(jnp.int32, (hd, LANES), 0)
                 == lax.broadcasted_iota(jnp.int32, (hd, LANES), 1)).astype(BF16)
        for g in groups:
            kaug_ref[g, :, 0:LANES] = extn_ref[...]
            for kt in range(seq // tk):
                rs = slice(kt * tk, (kt + 1) * tk)
                kaug_ref[g, rs, LANES:2 * LANES] = _dot(kv_ref[g, rs, :], place).astype(BF16)
                vst_ref[g, :, rs] = _transpose_bf16(eye, kv_ref[2 + g, rs, :])
                vwt_ref[g, :, rs] = _transpose_bf16(eye, kv_ref[6 + g, rs, :])

    col = lax.broadcasted_iota(jnp.int32, (1, m4), 1)
    t4 = i * tq + jnp.bitwise_and(col, tq - 1)
    t1 = i * tq + lax.broadcasted_iota(jnp.int32, (1, tq), 1)
    krow = lax.broadcasted_iota(jnp.int32, (tk, 1), 0)
    m0 = jnp.full((1, m4), NEG, F32)
    l0 = jnp.zeros((1, m4), F32)
    acc0 = jnp.zeros((hd, m4), F32)

    def attend(s, vt_tile, m_run, l_run, acc):
        m_new = jnp.maximum(m_run, jnp.max(s, axis=0, keepdims=True))
        alpha = jnp.exp2(m_run - m_new)
        e = jnp.exp2(s - m_new)
        l_new = alpha * l_run + jnp.sum(e, axis=0, keepdims=True)
        return m_new, l_new, alpha * acc + _dot(vt_tile, e.astype(BF16))

    qt = [jnp.concatenate([_transpose_bf16(eye, q_ref[NSA_HPG * g + h]) for h in range(NSA_HPG)], axis=1)
          for g in groups]

    tw = min(tq, WIN_TQ)
    n_w = WINDOW + tw
    win_bias, win_off = [], []
    for j in range(tq // tw):
        off = pl.multiple_of(jnp.maximum(i * tq + j * tw - WINDOW, 0), tw)
        kp = off + lax.broadcasted_iota(jnp.int32, (n_w, 1), 0)
        tj = t1[:, j * tw:(j + 1) * tw]
        bias = jnp.where((kp > tj - WINDOW) & (kp <= tj), 0.0, NEG)
        win_bias.append(jnp.concatenate([bias] * NSA_HPG, axis=1))
        win_off.append(off)

    def window(g):
        parts = []
        for j in range(tq // tw):
            cols = [qt[g][:, h * tq + j * tw:h * tq + (j + 1) * tw] for h in range(NSA_HPG)]
            s = _dot(kv_ref[4 + g, pl.ds(win_off[j], n_w), :], jnp.concatenate(cols, axis=1)) + win_bias[j]
            e = jnp.exp2(s - jnp.max(s, axis=0, keepdims=True))
            acc = _dot(vwt_ref[g, :, pl.ds(win_off[j], n_w)], e.astype(BF16))
            parts.append(acc / jnp.sum(e, axis=0, keepdims=True))
        return jnp.concatenate([p[:, h * tw:(h + 1) * tw] for h in range(NSA_HPG) for p in parts], axis=1)

    def compressed(g):
        ncmp = cmp_ref.shape[1]
        crow = lax.broadcasted_iota(jnp.int32, (ncmp, 1), 0)
        valid = (crow * CMP_STRIDE + (CMP_BLOCK - 1)) <= t4
        s = jnp.where(valid, _dot(cmp_ref[g], qt[g]), NEG)
        e = jnp.exp2(s - jnp.max(s, axis=0, keepdims=True))
        p = jnp.where(valid, e / jnp.sum(e, axis=0, keepdims=True), 0.0)
        o_cmp = _dot(_transpose_bf16(eye, cmp_ref[2 + g]), p.astype(BF16))
        psum = p[:, 0:tq] + p[:, tq:2 * tq] + p[:, 2 * tq:3 * tq] + p[:, 3 * tq:4 * tq]
        p_hi = psum.astype(BF16)
        p_lo = (psum - p_hi.astype(F32)).astype(BF16)
        imp = (_dot(ovt_ref[...], p_hi) + _dot(ovt_ref[...], p_lo))[0:n_sel, :]
        jrow = lax.broadcasted_iota(jnp.int32, (n_sel, 1), 0)
        cur = jnp.right_shift(t1, 6)
        forced = (jrow == 0) | (jrow == cur) | (jrow == cur - 1)
        vt = jnp.where(forced, 1e9, jnp.where(jrow * SEL_BLOCK <= t1, imp, -1.0))
        n_acc = 4
        cnt = [jnp.zeros((n_sel, tq), F32) for _ in range(n_acc)]
        for jp in range(n_sel):
            r = vt[jp:jp + 1, :]
            tie = jnp.where(jrow > jp, 1.0, 0.0)
            cnt[jp % n_acc] = cnt[jp % n_acc] + jnp.where(r > vt, 1.0, jnp.where(r == vt, tie, 0.0))
        rank = (cnt[0] + cnt[1]) + (cnt[2] + cnt[3])
        notsel = jnp.where(rank < float(SEL_TOPK), 0.0, 1.0)
        notsel = jnp.concatenate([notsel, jnp.zeros((LANES - n_sel, tq), F32)], axis=0).astype(BF16)
        qaug = jnp.concatenate([jnp.concatenate([notsel] * NSA_HPG, axis=1), qt[g], jnp.zeros((hd, m4), BF16)],
                               axis=0)
        return o_cmp, qaug

    o_win = [window(g) for g in groups]
    o_cmp, qaug = zip(*[compressed(g) for g in groups])

    def scores(g, kt):
        off = pl.multiple_of(kt * tk, tk)
        return _dot(kaug_ref[g, pl.ds(off, tk), :], qaug[g])

    def slc_body(kt, carry):
        off = pl.multiple_of(kt * tk, tk)
        return tuple(attend(scores(g, kt), vst_ref[g, :, pl.ds(off, tk)], *carry[g]) for g in groups)

    carry = lax.fori_loop(0, i, slc_body, tuple((m0, l0, acc0) for g in groups))
    causal = jnp.where((i * tk + krow) <= t1, 0.0, NEG)
    causal4 = jnp.concatenate([causal] * NSA_HPG, axis=1)
    off_d = pl.multiple_of(i * tk, tk)
    o_slc = []
    for g in groups:
        _, l_fin, acc_fin = attend(scores(g, i) + causal4, vst_ref[g, :, pl.ds(off_d, tk)], *carry[g])
        o_slc.append(acc_fin / l_fin)

    eye_q = (lax.broadcasted_iota(jnp.int32, (tq, tq), 0)
             == lax.broadcasted_iota(jnp.int32, (tq, tq), 1)).astype(BF16)
    o_heads = []
    for g in groups:
        gt = gate_ref[:, g * LANES:(g + 1) * LANES].T
        for h in range(NSA_HPG):
            cs = slice(h * tq, (h + 1) * tq)
            o = (gt[3 * h:3 * h + 1, :] * o_cmp[g][:, cs] + gt[3 * h + 1:3 * h + 2, :] * o_slc[g][:, cs]
                 + gt[3 * h + 2:3 * h + 3, :] * o_win[g][:, cs])
            o_heads.append(o.astype(BF16))
    o_ref[...] = _transpose_bf16(eye_q, jnp.concatenate(o_heads, axis=0))


def _nsa(q_hm, cmp, kv4, gates, ovt, extn, batch, seq):
    t = q_hm.shape[1]
    tq = NSA_TQ
    assert tq % min(tq, WIN_TQ) == 0 and WINDOW % min(tq, WIN_TQ) == 0 and WINDOW + tq <= seq
    nq = seq // tq
    ncmp = cmp.shape[2]
    qblk = pl.BlockSpec((NSA_HEADS, tq, NSA_HEAD_DIM), lambda b, i: (0, b * nq + i, 0))
    return pl.pallas_call(
        functools.partial(_nsa_kernel, tq=tq),
        out_shape=jax.ShapeDtypeStruct((t, NSA_WIDTH), BF16),
        grid=(batch, nq),
        in_specs=[qblk,
                  pl.BlockSpec((None, 4, ncmp, NSA_HEAD_DIM), lambda b, i: (b, 0, 0, 0)),
                  pl.BlockSpec((8, seq, NSA_HEAD_DIM), lambda b, i: (0, b, 0)),
                  pl.BlockSpec((tq, 2 * LANES), lambda b, i: (b * nq + i, 0)),
                  _const_spec(ovt.shape), _const_spec(extn.shape)],
        out_specs=pl.BlockSpec((tq, NSA_WIDTH), lambda b, i: (b * nq + i, 0)),
        scratch_shapes=[pltpu.VMEM((NSA_KV_HEADS, seq, 2 * LANES), BF16),
                        pltpu.VMEM((NSA_KV_HEADS, NSA_HEAD_DIM, seq), BF16),
                        pltpu.VMEM((NSA_KV_HEADS, NSA_HEAD_DIM, seq), BF16)],
        compiler_params=_params("arbitrary", "arbitrary"),
        name="nsa",
    )(q_hm, cmp, kv4, gates, ovt, extn)


def _merge_kernel(x_ref, pm_ref, on_ref, gm_ref, gpre_ref, wmg_ref, bmg_ref, wbp_ref, wbn_ref, wbg_ref,
                  wout_ref, gpost_ref, o_ref):
    x = x_ref[...]
    d = x.shape[1]
    h = _rms(x, gpre_ref[...]).astype(BF16)

    def gate(j):
        return jax.nn.sigmoid(_dot(h, wmg_ref[:, j * d:(j + 1) * d]) + bmg_ref[:, j * d:(j + 1) * d])

    comb = gate(0) * _dot(pm_ref[...], wbp_ref[...])
    comb = comb + gate(1) * _dot(on_ref[...], wbn_ref[...])
    comb = comb + gate(2) * _dot(gm_ref[...], wbg_ref[...])
    y = _dot(comb.astype(BF16), wout_ref[...])
    o_ref[...] = x + _rms(y, gpost_ref[...])


def _merge(x, l, pm, o_nsa, gm, gpre, wmg, bmg, wbp, wbn, wbg, wout, gpost):
    t, d = x.shape
    tm = MERGE_TM
    row = lambda width: pl.BlockSpec((tm, width), lambda i: (i, 0))
    cs = functools.partial(_const_spec, layer=l)
    return pl.pallas_call(
        _merge_kernel,
        out_shape=jax.ShapeDtypeStruct((t, d), F32),
        grid=(t // tm,),
        in_specs=[row(d), row(POOL_WIDTH), row(NSA_WIDTH), row(GMLP_WIDTH), cs((1, d)), cs((d, 3 * d)),
                  cs((1, 3 * d)), cs((POOL_WIDTH, d)), cs((NSA_WIDTH, d)), cs((GMLP_WIDTH, d)), cs((d, d)),
                  cs((1, d))],
        out_specs=row(d),
        compiler_params=_params("parallel"),
        name="merge",
    )(x, pm, o_nsa, gm, gpre, wmg, bmg, wbp, wbn, wbg, wout, gpost)


def _memkv_kernel(m_ref, g_ref, wk_ref, wv_ref, k_ref, v_ref):
    mn = _rms(m_ref[...], g_ref[...]).astype(BF16)
    k_ref[...] = _dot(mn, wk_ref[...]).astype(BF16)
    v_ref[...] = _dot(mn, wv_ref[...]).astype(BF16)


def _memkv(mem, l, g, wk, wv):
    batch, mlen, d = mem.shape
    blk = pl.BlockSpec((None, mlen, d), lambda b: (b, 0, 0))
    out = jax.ShapeDtypeStruct((batch, mlen, d), BF16)
    cs = functools.partial(_const_spec, layer=l)
    return pl.pallas_call(
        _memkv_kernel,
        out_shape=(out, out),
        grid=(batch,),
        in_specs=[blk, cs((1, d)), cs((d, d)), cs((d, d))],
        out_specs=(blk, blk),
        compiler_params=_params("parallel"),
        name="memkv",
    )(mem, g, wk, wv)


def _memattn_kernel(x_ref, gpre_ref, wq_ref, k_ref, v_ref, wo_ref, gpost_ref, o_ref):
    x = x_ref[...]
    d = x.shape[1]
    hd = d // MEM_HEADS
    h = _rms(x, gpre_ref[...]).astype(BF16)
    q = (_dot(h, wq_ref[...]) * (hd ** -0.5)).astype(BF16)
    outs = []
    for n in range(MEM_HEADS):
        cs = slice(n * hd, (n + 1) * hd)
        s = _dot_nt(q[:, cs], k_ref[:, cs])
        e = jnp.exp(s - jnp.max(s, axis=-1, keepdims=True))
        p = e / jnp.sum(e, axis=-1, keepdims=True)
        outs.append(_dot(p.astype(BF16), v_ref[:, cs]).astype(BF16))
    y = _dot(jnp.concatenate(outs, axis=1), wo_ref[...])
    o_ref[...] = x + _rms(y, gpost_ref[...])


def _memattn(x, l, gpre, wq, k, v, wo, gpost, batch, seq):
    t, d = x.shape
    mlen = k.shape[1]
    tm = MEM_TM
    nt = seq // tm
    row = pl.BlockSpec((tm, d), lambda b, i: (b * nt + i, 0))
    kvb = pl.BlockSpec((None, mlen, d), lambda b, i: (b, 0, 0))
    cs = functools.partial(_const_spec, layer=l)
    return pl.pallas_call(
        _memattn_kernel,
        out_shape=jax.ShapeDtypeStruct((t, d), F32),
        grid=(batch, nt),
        in_specs=[row, cs((1, d)), cs((d, d)), kvb, kvb, cs((d, d)), cs((1, d))],
        out_specs=row,
        compiler_params=_params("parallel", "arbitrary"),
        name="memattn",
    )(x, gpre, wq, k, v, wo, gpost)


def _overlap_matrix_t(n_cmp_pad, n_sel):
    cs = np.arange(n_cmp_pad)[None, :] * CMP_STRIDE
    ss = np.arange(LANES)[:, None] * SEL_BLOCK
    ov = np.minimum(cs + CMP_BLOCK, ss + SEL_BLOCK) - np.maximum(cs, ss)
    ov = np.maximum(ov, 0) / CMP_BLOCK
    ov[n_sel:, :] = 0.0
    return jnp.asarray(ov, dtype=BF16)


def _block_mask_matrix(seq):
    ex = (np.arange(seq)[:, None] // SEL_BLOCK) == np.arange(LANES)[None, :]
    return jnp.asarray(np.where(ex, -MASK_BIG, 0.0), dtype=BF16)


def kernel(x, mem, ff1_pre_g, ff1_w1, ff1_w3, ff1_w2, ff1_post_g, mix_pre_g, w_in, b_in, pool_w, pool_scale, cmp_pos_k, cmp_w1_k, cmp_w2_k, cmp_pos_v, cmp_w1_v, cmp_w2_v, gmlp_ln_g, gmlp_ln_b, gmlp_ws, gmlp_bs, w_br_pool, w_br_nsa, w_br_gmlp, w_mix_out, mix_post_g, mem_pre_g, mem_kv_g, mem_wq, mem_wk, mem_wv, mem_wo, mem_post_g, ff2_pre_g, ff2_w1, ff2_w3, ff2_w2, ff2_post_g):
    batch, seq, d = x.shape
    depth = w_in.shape[0]
    t = batch * seq
    n_cmp_rows = seq // CMP_STRIDE
    bf = lambda w: w.astype(BF16)
    vec = lambda g: g.reshape(depth, 1, -1)

    c_g = 1792
    c_gm = c_g + 3 * NSA_HEADS
    c_mg = c_gm + 2 * GMLP_WIDTH
    w_in_b = bf(w_in)
    zeros_w = lambda n: jnp.zeros((depth, d, n), BF16)
    zeros_b = lambda n: jnp.zeros((depth, n), b_in.dtype)
    half = 3 * NSA_HPG
    w_proj = jnp.concatenate(
        [w_in_b[:, :, :c_g], w_in_b[:, :, c_g:c_g + half], zeros_w(LANES - half),
         w_in_b[:, :, c_g + half:c_gm], zeros_w(LANES - half), w_in_b[:, :, c_gm:c_mg]], axis=2)
    b_proj = jnp.concatenate(
        [b_in[:, :c_g], b_in[:, c_g:c_g + half], zeros_b(LANES - half),
         b_in[:, c_g + half:c_gm], zeros_b(LANES - half), b_in[:, c_gm:c_mg]], axis=1).reshape(depth, 1, -1)
    w_mg = w_in_b[:, :, c_mg:]
    b_mg = b_in[:, c_mg:].reshape(depth, 1, -1)
    bsx = jnp.broadcast_to(jnp.swapaxes(gmlp_bs, 1, 2)[:, :, :, None],
                           (depth, GMLP_CHUNK, GMLP_GROUPS, GMLP_WIDTH // GMLP_GROUPS)).reshape(
                               depth, GMLP_CHUNK, GMLP_WIDTH)
    half_w = CMP_STRIDE * NSA_HEAD_DIM
    cmp_pos = jnp.stack([cmp_pos_k, cmp_pos_v], axis=1).reshape(depth, 2, 2, half_w)
    cmp_w1 = bf(jnp.stack([cmp_w1_k, cmp_w1_v], axis=1)).reshape(depth, 2, 2, half_w, -1)
    cmp_w2 = bf(jnp.stack([cmp_w2_k, cmp_w2_v], axis=1))
    w_br_nsa_b = bf(w_br_nsa)
    ff1 = (vec(ff1_pre_g), bf(ff1_w1), bf(ff1_w3), bf(ff1_w2), vec(ff1_post_g))
    ff2 = (vec(ff2_pre_g), bf(ff2_w1), bf(ff2_w3), bf(ff2_w2), vec(ff2_post_g))
    pool_w_b, pool_scale_v = bf(pool_w), vec(pool_scale)
    w_br_pool_b, w_br_gmlp_b, w_mix_out_b = bf(w_br_pool), bf(w_br_gmlp), bf(w_mix_out)
    mem_wq_b, mem_wk_b, mem_wv_b, mem_wo_b = bf(mem_wq), bf(mem_wk), bf(mem_wv), bf(mem_wo)
    mix_pre, mix_post = vec(mix_pre_g), vec(mix_post_g)
    mem_pre, mem_kv, mem_post = vec(mem_pre_g), vec(mem_kv_g), vec(mem_post_g)
    ln_g, ln_b = vec(gmlp_ln_g), vec(gmlp_ln_b)
    ov = _overlap_matrix_t(n_cmp_rows, seq // SEL_BLOCK)
    ex = _block_mask_matrix(seq)

    xs = x.reshape(t, d)
    for l in range(depth):
        xs = _ffn(xs, l, *ff1)
        a, q_hm, kvc, kv4, gates, gm = _inproj(xs, l, mix_pre, w_proj, b_proj, gmlp_ws, ln_g, ln_b, bsx)
        pm = _pool(a, l, pool_w_b, pool_scale_v, batch, seq)
        a4 = kvc.reshape(batch, n_cmp_rows, CMP_STRIDE, 4, NSA_HEAD_DIM).transpose(0, 3, 1, 2, 4).reshape(
            batch, 4, n_cmp_rows, half_w)
        cmp = _cmpkv(a4, l, cmp_pos, cmp_w1, cmp_w2)
        o_nsa = _nsa(q_hm, cmp, kv4, gates, ov, ex, batch, seq)
        xs = _merge(xs, l, pm, o_nsa, gm, mix_pre, w_mg, b_mg, w_br_pool_b, w_br_nsa_b, w_br_gmlp_b,
                    w_mix_out_b, mix_post)
        mk, mv = _memkv(mem, l, mem_kv, mem_wk_b, mem_wv_b)
        xs = _memattn(xs, l, mem_pre, mem_wq_b, mk, mv, mem_wo_b, mem_post, batch, seq)
        xs = _ffn(xs, l, *ff2)
    return xs.reshape(batch, seq, d)
```

```python
import functools

import numpy as np
import jax
import jax.numpy as jnp
from jax import lax
from jax.experimental import pallas as pl
from jax.experimental.pallas import tpu as pltpu

F32 = jnp.float32
BF16 = jnp.bfloat16

EPS = 1e-6
NEG = -1e30
MASK_BIG = 2.0 ** 100
LOG2E = float(np.log2(np.e))

MEM_HEADS = 4
POOL_WINDOWS = (2, 4, 8, 16)
POOL_GROUP = 128
NSA_HEADS = 8
NSA_KV_HEADS = 2
NSA_HPG = NSA_HEADS // NSA_KV_HEADS
NSA_HEAD_DIM = 64
CMP_BLOCK = 32
CMP_STRIDE = 16
SEL_BLOCK = 64
SEL_TOPK = 8
WINDOW = 256
GMLP_WIDTH = 512
GMLP_GROUPS = 4
GMLP_CHUNK = 128
POOL_WIDTH = 512
NSA_WIDTH = NSA_HEADS * NSA_HEAD_DIM

LANES = 128
VMEM_LIMIT_BYTES = 56 * 1024 * 1024

FFN_TM = 512
FFN_CHUNK = 512
PROJ_TM = 512
MERGE_TM = 512
MEM_TM = 512
NSA_TQ = 512
WIN_TQ = 128


def _dot(a, b):
    return jnp.dot(a, b, preferred_element_type=F32)


def _dot_nt(a, b):
    return lax.dot_general(a, b, (((1,), (1,)), ((), ())), preferred_element_type=F32)


def _rms(x, g):
    return x * lax.rsqrt(jnp.mean(x * x, axis=-1, keepdims=True) + EPS) * g


def _gelu(x):
    c = np.float32(np.sqrt(2.0 / np.pi))
    return x * (0.5 * (1.0 + jnp.tanh(c * (x + 0.044715 * (x * x * x)))))


def _const_spec(shape, layer=None):
    n = len(shape)
    if layer is None:
        return pl.BlockSpec(shape, lambda *_: (0,) * n, pipeline_mode=pl.Buffered(1))
    return pl.BlockSpec((None,) + tuple(shape), lambda *_: (layer,) + (0,) * n, pipeline_mode=pl.Buffered(1))


def _params(*sem):
    return pltpu.CompilerParams(dimension_semantics=sem, vmem_limit_bytes=VMEM_LIMIT_BYTES)


def _ffn_kernel(x_ref, gpre_ref, w1_ref, w3_ref, w2_ref, gpost_ref, o_ref, h_ref, acc_ref, *, chunks):
    x = x_ref[...]
    h_ref[...] = _rms(x, gpre_ref[...]).astype(BF16)
    for n, (off, width) in enumerate(chunks):
        h = h_ref[...]
        a = _dot(h, w1_ref[:, off:off + width])
        b = _dot(h, w3_ref[:, off:off + width])
        u = (a * jax.nn.sigmoid(a) * b).astype(BF16)
        y = _dot(u, w2_ref[off:off + width, :])
        if n == 0:
            acc_ref[...] = y
        else:
            acc_ref[...] += y
    o_ref[...] = x + 0.5 * _rms(acc_ref[...], gpost_ref[...])


def _ffn(x, l, gpre, w1, w3, w2, gpost):
    t, d = x.shape
    f = w1.shape[-1]
    tm = FFN_TM
    cs = functools.partial(_const_spec, layer=l)
    chunks = tuple((off, min(FFN_CHUNK, f - off)) for off in range(0, f, FFN_CHUNK))
    row = pl.BlockSpec((tm, d), lambda i: (i, 0))
    return pl.pallas_call(
        functools.partial(_ffn_kernel, chunks=chunks),
        out_shape=jax.ShapeDtypeStruct((t, d), F32),
        grid=(t // tm,),
        in_specs=[row, cs((1, d)), cs((d, f)), cs((d, f)), cs((f, d)), cs((1, d))],
        out_specs=row,
        scratch_shapes=[pltpu.VMEM((tm, d), BF16), pltpu.VMEM((tm, d), F32)],
        compiler_params=_params("parallel"),
        name="ffn",
    )(x, gpre, w1, w3, w2, gpost)


_A0, _Q0, _KV0, _G0, _GM0, _PROJ_COLS = 0, 512, 1024, 1792, 2048, 3072


def _inproj_kernel(x_ref, g_ref, w_ref, b_ref, ws_ref, lng_ref, lnb_ref, bsx_ref,
                   a_ref, q_ref, kvc_ref, kv4_ref, gate_ref, gm_ref, *, tm):
    h = _rms(x_ref[...], g_ref[...]).astype(BF16)
    z = _dot(h, w_ref[...]) + b_ref[...]
    a_ref[...] = z[:, _A0:_A0 + POOL_WIDTH]
    zq = z[:, _Q0:_Q0 + NSA_WIDTH] * (NSA_HEAD_DIM ** -0.5 * LOG2E)
    for hh in range(NSA_HEADS):
        q_ref[hh] = zq[:, hh * 64:(hh + 1) * 64].astype(BF16)
    kvc_ref[0] = z[:, _KV0:_KV0 + LANES]
    kvc_ref[1] = z[:, _KV0 + LANES:_KV0 + 2 * LANES]
    for r in range(8):
        c0 = _KV0 + 256 + r * 64
        kv4_ref[r] = z[:, c0:c0 + 64].astype(BF16)
    gate_ref[...] = jax.nn.sigmoid(z[:, _G0:_G0 + 256])
    gm = _gelu(z[:, _GM0:_GM0 + 2 * GMLP_WIDTH])
    u = gm[:, :GMLP_WIDTH]
    v = gm[:, GMLP_WIDTH:]
    mu = jnp.mean(v, axis=-1, keepdims=True)
    vc = v - mu
    var = jnp.mean(vc * vc, axis=-1, keepdims=True)
    vn = (vc * lax.rsqrt(var + EPS) * lng_ref[...] + lnb_ref[...]).astype(BF16)
    ri = lax.broadcasted_iota(jnp.int32, (GMLP_CHUNK, GMLP_CHUNK), 0)
    ci = lax.broadcasted_iota(jnp.int32, (GMLP_CHUNK, GMLP_CHUNK), 1)
    for g in range(GMLP_GROUPS):
        wg = jnp.where(ri >= ci, ws_ref[g], 0.0).astype(BF16)
        cs = slice(g * 128, (g + 1) * 128)
        for c in range(tm // GMLP_CHUNK):
            rs = slice(c * GMLP_CHUNK, (c + 1) * GMLP_CHUNK)
            s = _dot(wg, vn[rs, cs]) + bsx_ref[:, cs]
            gm_ref[rs, cs] = (u[rs, cs] * s).astype(BF16)


def _inproj(x, l, g, w, b, ws, lng, lnb, bsx):
    t, d = x.shape
    tm = PROJ_TM
    cs = functools.partial(_const_spec, layer=l)
    row = lambda width: pl.BlockSpec((tm, width), lambda i: (i, 0))
    heads = pl.BlockSpec((8, tm, 64), lambda i: (0, i, 0))
    return pl.pallas_call(
        functools.partial(_inproj_kernel, tm=tm),
        out_shape=(
            jax.ShapeDtypeStruct((t, POOL_WIDTH), F32),
            jax.ShapeDtypeStruct((8, t, 64), BF16),
            jax.ShapeDtypeStruct((2, t, LANES), F32),
            jax.ShapeDtypeStruct((8, t, 64), BF16),
            jax.ShapeDtypeStruct((t, 256), F32),
            jax.ShapeDtypeStruct((t, GMLP_WIDTH), BF16),
        ),
        grid=(t // tm,),
        in_specs=[row(d), cs((1, d)), cs((d, _PROJ_COLS)), cs((1, _PROJ_COLS)),
                  cs((GMLP_GROUPS, GMLP_CHUNK, GMLP_CHUNK)), cs((1, GMLP_WIDTH)), cs((1, GMLP_WIDTH)),
                  cs((GMLP_CHUNK, GMLP_WIDTH))],
        out_specs=(row(POOL_WIDTH), heads, pl.BlockSpec((2, tm, LANES), lambda i: (0, i, 0)), heads, row(256),
                   row(GMLP_WIDTH)),
        compiler_params=_params("parallel"),
        name="inproj",
    )(x, g, w, b, ws, lng, lnb, bsx)


_POOL_PAD = 16


def _pool_kernel(a_ref, pw_ref, ps_ref, o_ref, pad_ref, *, seq):
    pos = lax.broadcasted_iota(jnp.int32, (seq, 1), 0).astype(F32) + 1.0
    pad_ref[0:_POOL_PAD, :] = jnp.zeros((_POOL_PAD, POOL_GROUP), F32)
    for gi, w in enumerate(POOL_WINDOWS):
        cs = slice(gi * POOL_GROUP, (gi + 1) * POOL_GROUP)
        x = a_ref[:, cs]
        s = x
        k = 1
        while k < w:
            pad_ref[_POOL_PAD:_POOL_PAD + seq, :] = s
            s = s + pad_ref[_POOL_PAD - k:_POOL_PAD - k + seq, :]
            k *= 2
        pooled = (s / jnp.minimum(pos, float(w)) - x).astype(BF16)
        o_ref[:, cs] = (_dot(pooled, pw_ref[gi]) * ps_ref[:, cs]).astype(BF16)


def _pool(a, l, pw, ps, batch, seq):
    t = a.shape[0]
    blk = pl.BlockSpec((seq, POOL_WIDTH), lambda b: (b, 0))
    cs = functools.partial(_const_spec, layer=l)
    return pl.pallas_call(
        functools.partial(_pool_kernel, seq=seq),
        out_shape=jax.ShapeDtypeStruct((t, POOL_WIDTH), BF16),
        grid=(batch,),
        in_specs=[blk, cs((len(POOL_WINDOWS), POOL_GROUP, POOL_GROUP)), cs((1, POOL_WIDTH))],
        out_specs=blk,
        scratch_shapes=[pltpu.VMEM((_POOL_PAD + seq, POOL_GROUP), F32)],
        compiler_params=_params("parallel"),
        name="pool",
    )(a, pw, ps)


def _cmpkv_kernel(x_ref, pos_ref, w1_ref, w2_ref, o_ref):
    nrow = x_ref.shape[0] // CMP_STRIDE
    p = q = None
    for l in range(CMP_STRIDE):
        xl = x_ref[pl.ds(l, nrow, stride=CMP_STRIDE), :]
        pl_ = _dot((xl + pos_ref[l:l + 1, :]).astype(BF16), w1_ref[l])
        ql = _dot((xl + pos_ref[CMP_STRIDE + l:CMP_STRIDE + l + 1, :]).astype(BF16), w1_ref[CMP_STRIDE + l])
        p = pl_ if p is None else p + pl_
        q = ql if q is None else q + ql
    hpre = p + pltpu.roll(q, nrow - 1, axis=0)
    h = _gelu(hpre).astype(BF16)
    hid = w2_ref.shape[0]
    for g in range(NSA_KV_HEADS):
        o_ref[g] = _dot(h[:, g * hid:(g + 1) * hid], w2_ref[...]).astype(BF16)


def _cmpkv(kvc, l, pos, w1, w2, batch, seq):
    nrow = seq // CMP_STRIDE
    hid = w2.shape[-2]
    return pl.pallas_call(
        _cmpkv_kernel,
        out_shape=jax.ShapeDtypeStruct((batch, 2 * NSA_KV_HEADS, nrow, NSA_HEAD_DIM), BF16),
        grid=(batch, 2),
        in_specs=[pl.BlockSpec((None, seq, LANES), lambda b, r: (r, b, 0)),
                  pl.BlockSpec((None, None, CMP_BLOCK, LANES), lambda b, r: (l, r, 0, 0)),
                  pl.BlockSpec((None, None, CMP_BLOCK, LANES, NSA_KV_HEADS * hid), lambda b, r: (l, r, 0, 0, 0)),
                  pl.BlockSpec((None, None, hid, NSA_HEAD_DIM), lambda b, r: (l, r, 0, 0))],
        out_specs=pl.BlockSpec((None, NSA_KV_HEADS, nrow, NSA_HEAD_DIM), lambda b, r: (b, r, 0, 0)),
        compiler_params=_params("parallel", "arbitrary"),
        name="cmpkv",
    )(kvc, pos, w1, w2)


def _transpose_bf16(eye, x):
    return _dot_nt(eye, x).astype(BF16)


def _nsa_kernel(q_ref, cmp_ref, kv_ref, gate_ref, ovt_ref, extn_ref, o_ref, kaug_ref, vst_ref, vwt_ref, *, tq):
    i = pl.program_id(1)
    tk = tq
    hd = NSA_HEAD_DIM
    m4 = NSA_HPG * tq
    seq = kv_ref.shape[1]
    n_sel = seq // SEL_BLOCK
    groups = range(NSA_KV_HEADS)
    eye = (lax.broadcasted_iota(jnp.int32, (hd, hd), 0)
           == lax.broadcasted_iota(jnp.int32, (hd, hd), 1)).astype(BF16)

    @pl.when(i == 0)
    def _():
        place = (lax.broadcasted_iota(jnp.int32, (hd, LANES), 0)
                 == lax.broadcasted_iota(jnp.int32, (hd, LANES), 1)).astype(BF16)
        for g in groups:
            kaug_ref[g, :, 0:LANES] = extn_ref[...]
            for kt in range(seq // tk):
                rs = slice(kt * tk, (kt + 1) * tk)
                kaug_ref[g, rs, LANES:2 * LANES] = _dot(kv_ref[g, rs, :], place).astype(BF16)
                vst_ref[g, :, rs] = _transpose_bf16(eye, kv_ref[2 + g, rs, :])
                vwt_ref[g, :, rs] = _transpose_bf16(eye, kv_ref[6 + g, rs, :])

    col = lax.broadcasted_iota(jnp.int32, (1, m4), 1)
    t4 = i * tq + jnp.bitwise_and(col, tq - 1)
    t1 = i * tq + lax.broadcasted_iota(jnp.int32, (1, tq), 1)
    krow = lax.broadcasted_iota(jnp.int32, (tk, 1), 0)
    m0 = jnp.full((1, m4), NEG, F32)
    l0 = jnp.zeros((1, m4), F32)
    acc0 = jnp.zeros((hd, m4), F32)

    def attend(s, vt_tile, m_run, l_run, acc):
        m_new = jnp.maximum(m_run, jnp.max(s, axis=0, keepdims=True))
        alpha = jnp.exp2(m_run - m_new)
        e = jnp.exp2(s - m_new)
        l_new = alpha * l_run + jnp.sum(e, axis=0, keepdims=True)
        return m_new, l_new, alpha * acc + _dot(vt_tile, e.astype(BF16))

    qt = [jnp.concatenate([_transpose_bf16(eye, q_ref[NSA_HPG * g + h]) for h in range(NSA_HPG)], axis=1)
          for g in groups]

    tw = min(tq, WIN_TQ)
    n_w = WINDOW + tw
    win_bias, win_off = [], []
    for j in range(tq // tw):
        off = pl.multiple_of(jnp.maximum(i * tq + j * tw - WINDOW, 0), tw)
        kp = off + lax.broadcasted_iota(jnp.int32, (n_w, 1), 0)
        tj = t1[:, j * tw:(j + 1) * tw]
        bias = jnp.where((kp > tj - WINDOW) & (kp <= tj), 0.0, NEG)
        win_bias.append(jnp.concatenate([bias] * NSA_HPG, axis=1))
        win_off.append(off)

    def window(g):
        parts = []
        for j in range(tq // tw):
            cols = [qt[g][:, h * tq + j * tw:h * tq + (j + 1) * tw] for h in range(NSA_HPG)]
            s = _dot(kv_ref[4 + g, pl.ds(win_off[j], n_w), :], jnp.concatenate(cols, axis=1)) + win_bias[j]
            e = jnp.exp2(s - jnp.max(s, axis=0, keepdims=True))
            acc = _dot(vwt_ref[g, :, pl.ds(win_off[j], n_w)], e.astype(BF16))
            parts.append(acc / jnp.sum(e, axis=0, keepdims=True))
        return jnp.concatenate([p[:, h * tw:(h + 1) * tw] for h in range(NSA_HPG) for p in parts], axis=1)

    def compressed(g):
        ncmp = cmp_ref.shape[1]
        crow = lax.broadcasted_iota(jnp.int32, (ncmp, 1), 0)
        valid = (crow * CMP_STRIDE + (CMP_BLOCK - 1)) <= t4
        s = jnp.where(valid, _dot(cmp_ref[g], qt[g]), NEG)
        e = jnp.exp2(s - jnp.max(s, axis=0, keepdims=True))
        p = jnp.where(valid, e / jnp.sum(e, axis=0, keepdims=True), 0.0)
        o_cmp = _dot(_transpose_bf16(eye, cmp_ref[2 + g]), p.astype(BF16))
        psum = p[:, 0:tq] + p[:, tq:2 * tq] + p[:, 2 * tq:3 * tq] + p[:, 3 * tq:4 * tq]
        p_hi = psum.astype(BF16)
        p_lo = (psum - p_hi.astype(F32)).astype(BF16)
        imp = (_dot(ovt_ref[...], p_hi) + _dot(ovt_ref[...], p_lo))[0:n_sel, :]
        jrow = lax.broadcasted_iota(jnp.int32, (n_sel, 1), 0)
        cur = jnp.right_shift(t1, 6)
        forced = (jrow == 0) | (jrow == cur) | (jrow == cur - 1)
        vt = jnp.where(forced, 1e9, jnp.where(jrow * SEL_BLOCK <= t1, imp, -1.0))
        n_acc = 4
        cnt = [jnp.zeros((n_sel, tq), F32) for _ in range(n_acc)]
        for jp in range(n_sel):
            r = vt[jp:jp + 1, :]
            tie = jnp.where(jrow > jp, 1.0, 0.0)
            cnt[jp % n_acc] = cnt[jp % n_acc] + jnp.where(r > vt, 1.0, jnp.where(r == vt, tie, 0.0))
        rank = (cnt[0] + cnt[1]) + (cnt[2] + cnt[3])
        notsel = jnp.where(rank < float(SEL_TOPK), 0.0, 1.0)
        notsel = jnp.concatenate([notsel, jnp.zeros((LANES - n_sel, tq), F32)], axis=0).astype(BF16)
        qaug = jnp.concatenate([jnp.concatenate([notsel] * NSA_HPG, axis=1), qt[g], jnp.zeros((hd, m4), BF16)],
                               axis=0)
        return o_cmp, qaug

    o_win = [window(g) for g in groups]
    o_cmp, qaug = zip(*[compressed(g) for g in groups])

    def scores(g, kt):
        off = pl.multiple_of(kt * tk, tk)
        return _dot(kaug_ref[g, pl.ds(off, tk), :], qaug[g])

    def slc_body(kt, carry):
        off = pl.multiple_of(kt * tk, tk)
        return tuple(attend(scores(g, kt), vst_ref[g, :, pl.ds(off, tk)], *carry[g]) for g in groups)

    carry = lax.fori_loop(0, i, slc_body, tuple((m0, l0, acc0) for g in groups))
    causal = jnp.where((i * tk + krow) <= t1, 0.0, NEG)
    causal4 = jnp.concatenate([causal] * NSA_HPG, axis=1)
    off_d = pl.multiple_of(i * tk, tk)
    o_slc = []
    for g in groups:
        _, l_fin, acc_fin = attend(scores(g, i) + causal4, vst_ref[g, :, pl.ds(off_d, tk)], *carry[g])
        o_slc.append(acc_fin / l_fin)

    eye_q = (lax.broadcasted_iota(jnp.int32, (tq, tq), 0)
             == lax.broadcasted_iota(jnp.int32, (tq, tq), 1)).astype(BF16)
    o_heads = []
    for g in groups:
        gt = gate_ref[:, g * LANES:(g + 1) * LANES].T
        for h in range(NSA_HPG):
            cs = slice(h * tq, (h + 1) * tq)
            o = (gt[3 * h:3 * h + 1, :] * o_cmp[g][:, cs] + gt[3 * h + 1:3 * h + 2, :] * o_slc[g][:, cs]
                 + gt[3 * h + 2:3 * h + 3, :] * o_win[g][:, cs])
            o_heads.append(o.astype(BF16))
    o_ref[...] = _transpose_bf16(eye_q, jnp.concatenate(o_heads, axis=0))


def _nsa(q_hm, cmp, kv4, gates, ovt, extn, batch, seq):
    t = q_hm.shape[1]
    tq = NSA_TQ
    assert tq % min(tq, WIN_TQ) == 0 and WINDOW % min(tq, WIN_TQ) == 0 and WINDOW + tq <= seq
    nq = seq // tq
    ncmp = cmp.shape[2]
    qblk = pl.BlockSpec((NSA_HEADS, tq, NSA_HEAD_DIM), lambda b, i: (0, b * nq + i, 0))
    return pl.pallas_call(
        functools.partial(_nsa_kernel, tq=tq),
        out_shape=jax.ShapeDtypeStruct((t, NSA_WIDTH), BF16),
        grid=(batch, nq),
        in_specs=[qblk,
                  pl.BlockSpec((None, 4, ncmp, NSA_HEAD_DIM), lambda b, i: (b, 0, 0, 0)),
                  pl.BlockSpec((8, seq, NSA_HEAD_DIM), lambda b, i: (0, b, 0)),
                  pl.BlockSpec((tq, 2 * LANES), lambda b, i: (b * nq + i, 0)),
                  _const_spec(ovt.shape), _const_spec(extn.shape)],
        out_specs=pl.BlockSpec((tq, NSA_WIDTH), lambda b, i: (b * nq + i, 0)),
        scratch_shapes=[pltpu.VMEM((NSA_KV_HEADS, seq, 2 * LANES), BF16),
                        pltpu.VMEM((NSA_KV_HEADS, NSA_HEAD_DIM, seq), BF16),
                        pltpu.VMEM((NSA_KV_HEADS, NSA_HEAD_DIM, seq), BF16)],
        compiler_params=_params("arbitrary", "arbitrary"),
        name="nsa",
    )(q_hm, cmp, kv4, gates, ovt, extn)


def _merge_kernel(x_ref, pm_ref, on_ref, gm_ref, gpre_ref, wmg_ref, bmg_ref, wbp_ref, wbn_ref, wbg_ref,
                  wout_ref, gpost_ref, o_ref):
    x = x_ref[...]
    d = x.shape[1]
    h = _rms(x, gpre_ref[...]).astype(BF16)

    def gate(j):
        return jax.nn.sigmoid(_dot(h, wmg_ref[:, j * d:(j + 1) * d]) + bmg_ref[:, j * d:(j + 1) * d])

    comb = gate(0) * _dot(pm_ref[...], wbp_ref[...])
    comb = comb + gate(1) * _dot(on_ref[...], wbn_ref[...])
    comb = comb + gate(2) * _dot(gm_ref[...], wbg_ref[...])
    y = _dot(comb.astype(BF16), wout_ref[...])
    o_ref[...] = x + _rms(y, gpost_ref[...])


def _merge(x, l, pm, o_nsa, gm, gpre, wmg, bmg, wbp, wbn, wbg, wout, gpost):
    t, d = x.shape
    tm = MERGE_TM
    row = lambda width: pl.BlockSpec((tm, width), lambda i: (i, 0))
    cs = functools.partial(_const_spec, layer=l)
    return pl.pallas_call(
        _merge_kernel,
        out_shape=jax.ShapeDtypeStruct((t, d), F32),
        grid=(t // tm,),
        in_specs=[row(d), row(POOL_WIDTH), row(NSA_WIDTH), row(GMLP_WIDTH), cs((1, d)), cs((d, 3 * d)),
                  cs((1, 3 * d)), cs((POOL_WIDTH, d)), cs((NSA_WIDTH, d)), cs((GMLP_WIDTH, d)), cs((d, d)),
                  cs((1, d))],
        out_specs=row(d),
        compiler_params=_params("parallel"),
        name="merge",
    )(x, pm, o_nsa, gm, gpre, wmg, bmg, wbp, wbn, wbg, wout, gpost)


def _memkv_kernel(m_ref, g_ref, wk_ref, wv_ref, k_ref, v_ref):
    mn = _rms(m_ref[...], g_ref[...]).astype(BF16)
    k_ref[...] = _dot(mn, wk_ref[...]).astype(BF16)
    v_ref[...] = _dot(mn, wv_ref[...]).astype(BF16)


def _memkv(mem, l, g, wk, wv):
    batch, mlen, d = mem.shape
    blk = pl.BlockSpec((None, mlen, d), lambda b: (b, 0, 0))
    out = jax.ShapeDtypeStruct((batch, mlen, d), BF16)
    cs = functools.partial(_const_spec, layer=l)
    return pl.pallas_call(
        _memkv_kernel,
        out_shape=(out, out),
        grid=(batch,),
        in_specs=[blk, cs((1, d)), cs((d, d)), cs((d, d))],
        out_specs=(blk, blk),
        compiler_params=_params("parallel"),
        name="memkv",
    )(mem, g, wk, wv)


def _memattn_kernel(x_ref, gpre_ref, wq_ref, k_ref, v_ref, wo_ref, gpost_ref, o_ref):
    x = x_ref[...]
    d = x.shape[1]
    hd = d // MEM_HEADS
    h = _rms(x, gpre_ref[...]).astype(BF16)
    q = (_dot(h, wq_ref[...]) * (hd ** -0.5)).astype(BF16)
    outs = []
    for n in range(MEM_HEADS):
        cs = slice(n * hd, (n + 1) * hd)
        s = _dot_nt(q[:, cs], k_ref[:, cs])
        e = jnp.exp(s - jnp.max(s, axis=-1, keepdims=True))
        p = e / jnp.sum(e, axis=-1, keepdims=True)
        outs.append(_dot(p.astype(BF16), v_ref[:, cs]).astype(BF16))
    y = _dot(jnp.concatenate(outs, axis=1), wo_ref[...])
    o_ref[...] = x + _rms(y, gpost_ref[...])


def _memattn(x, l, gpre, wq, k, v, wo, gpost, batch, seq):
    t, d = x.shape
    mlen = k.shape[1]
    tm = MEM_TM
    nt = seq // tm
    row = pl.BlockSpec((tm, d), lambda b, i: (b * nt + i, 0))
    kvb = pl.BlockSpec((None, mlen, d), lambda b, i: (b, 0, 0))
    cs = functools.partial(_const_spec, layer=l)
    return pl.pallas_call(
        _memattn_kernel,
        out_shape=jax.ShapeDtypeStruct((t, d), F32),
        grid=(batch, nt),
        in_specs=[row, cs((1, d)), cs((d, d)), kvb, kvb, cs((d, d)), cs((1, d))],
        out_specs=row,
        compiler_params=_params("parallel", "arbitrary"),
        name="memattn",
    )(x, gpre, wq, k, v, wo, gpost)


def _overlap_matrix_t(n_cmp_pad, n_sel):
    cs = np.arange(n_cmp_pad)[None, :] * CMP_STRIDE
    ss = np.arange(LANES)[:, None] * SEL_BLOCK
    ov = np.minimum(cs + CMP_BLOCK, ss + SEL_BLOCK) - np.maximum(cs, ss)
    ov = np.maximum(ov, 0) / CMP_BLOCK
    ov[n_sel:, :] = 0.0
    return jnp.asarray(ov, dtype=BF16)


def _block_mask_matrix(seq):
    ex = (np.arange(seq)[:, None] // SEL_BLOCK) == np.arange(LANES)[None, :]
    return jnp.asarray(np.where(ex, -MASK_BIG, 0.0), dtype=BF16)


def kernel(x, mem, ff1_pre_g, ff1_w1, ff1_w3, ff1_w2, ff1_post_g, mix_pre_g, w_in, b_in, pool_w, pool_scale, cmp_pos_k, cmp_w1_k, cmp_w2_k, cmp_pos_v, cmp_w1_v, cmp_w2_v, gmlp_ln_g, gmlp_ln_b, gmlp_ws, gmlp_bs, w_br_pool, w_br_nsa, w_br_gmlp, w_mix_out, mix_post_g, mem_pre_g, mem_kv_g, mem_wq, mem_wk, mem_wv, mem_wo, mem_post_g, ff2_pre_g, ff2_w1, ff2_w3, ff2_w2, ff2_post_g):
    batch, seq, d = x.shape
    depth = w_in.shape[0]
    t = batch * seq
    n_cmp_rows = seq // CMP_STRIDE
    bf = lambda w: w.astype(BF16)
    vec = lambda g: g.reshape(depth, 1, -1)

    c_g = 1792
    c_gm = c_g + 3 * NSA_HEADS
    c_mg = c_gm + 2 * GMLP_WIDTH
    w_in_b = bf(w_in)
    zeros_w = lambda n: jnp.zeros((depth, d, n), BF16)
    zeros_b = lambda n: jnp.zeros((depth, n), b_in.dtype)
    half = 3 * NSA_HPG
    w_proj = jnp.concatenate(
        [w_in_b[:, :, :c_g], w_in_b[:, :, c_g:c_g + half], zeros_w(LANES - half),
         w_in_b[:, :, c_g + half:c_gm], zeros_w(LANES - half), w_in_b[:, :, c_gm:c_mg]], axis=2)
    b_proj = jnp.concatenate(
        [b_in[:, :c_g], b_in[:, c_g:c_g + half], zeros_b(LANES - half),
         b_in[:, c_g + half:c_gm], zeros_b(LANES - half), b_in[:, c_gm:c_mg]], axis=1).reshape(depth, 1, -1)
    w_mg = w_in_b[:, :, c_mg:]
    b_mg = b_in[:, c_mg:].reshape(depth, 1, -1)
    bsx = jnp.broadcast_to(jnp.swapaxes(gmlp_bs, 1, 2)[:, :, :, None],
                           (depth, GMLP_CHUNK, GMLP_GROUPS, GMLP_WIDTH // GMLP_GROUPS)).reshape(
                               depth, GMLP_CHUNK, GMLP_WIDTH)
    cmp_pos = jnp.stack([cmp_pos_k, cmp_pos_v], axis=1)
    cmp_pos = jnp.concatenate([cmp_pos, cmp_pos], axis=-1)
    cmp_w1 = bf(jnp.stack([cmp_w1_k, cmp_w1_v], axis=1))
    w1_zero = jnp.zeros_like(cmp_w1)
    cmp_w1 = jnp.concatenate([jnp.concatenate([cmp_w1, w1_zero], axis=-1),
                              jnp.concatenate([w1_zero, cmp_w1], axis=-1)], axis=-2)
    cmp_w2 = bf(jnp.stack([cmp_w2_k, cmp_w2_v], axis=1))
    w_br_nsa_b = bf(w_br_nsa)
    ff1 = (vec(ff1_pre_g), bf(ff1_w1), bf(ff1_w3), bf(ff1_w2), vec(ff1_post_g))
    ff2 = (vec(ff2_pre_g), bf(ff2_w1), bf(ff2_w3), bf(ff2_w2), vec(ff2_post_g))
    pool_w_b, pool_scale_v = bf(pool_w), vec(pool_scale)
    w_br_pool_b, w_br_gmlp_b, w_mix_out_b = bf(w_br_pool), bf(w_br_gmlp), bf(w_mix_out)
    mem_wq_b, mem_wk_b, mem_wv_b, mem_wo_b = bf(mem_wq), bf(mem_wk), bf(mem_wv), bf(mem_wo)
    mix_pre, mix_post = vec(mix_pre_g), vec(mix_post_g)
    mem_pre, mem_kv, mem_post = vec(mem_pre_g), vec(mem_kv_g), vec(mem_post_g)
    ln_g, ln_b = vec(gmlp_ln_g), vec(gmlp_ln_b)
    ov = _overlap_matrix_t(n_cmp_rows, seq // SEL_BLOCK)
    ex = _block_mask_matrix(seq)

    xs = x.reshape(t, d)
    for l in range(depth):
        xs = _ffn(xs, l, *ff1)
        a, q_hm, kvc, kv4, gates, gm = _inproj(xs, l, mix_pre, w_proj, b_proj, gmlp_ws, ln_g, ln_b, bsx)
        pm = _pool(a, l, pool_w_b, pool_scale_v, batch, seq)
        cmp = _cmpkv(kvc, l, cmp_pos, cmp_w1, cmp_w2, batch, seq)
        o_nsa = _nsa(q_hm, cmp, kv4, gates, ov, ex, batch, seq)
        xs = _merge(xs, l, pm, o_nsa, gm, mix_pre, w_mg, b_mg, w_br_pool_b, w_br_nsa_b, w_br_gmlp_b,
                    w_mix_out_b, mix_post)
        mk, mv = _memkv(mem, l, mem_kv, mem_wk_b, mem_wv_b)
        xs = _memattn(xs, l, mem_pre, mem_wq_b, mk, mv, mem_wo_b, mem_post, batch, seq)
        xs = _ffn(xs, l, *ff2)
    return xs.reshape(batch, seq, d)
```

```python
import functools

import numpy as np
import jax
import jax.numpy as jnp
from jax import lax
from jax.experimental import pallas as pl
from jax.experimental.pallas import tpu as pltpu

F32 = jnp.float32
BF16 = jnp.bfloat16

EPS = 1e-6
NEG = -1e30
MASK_BIG = 2.0 ** 100
LOG2E = float(np.log2(np.e))

MEM_HEADS = 4
POOL_WINDOWS = (2, 4, 8, 16)
POOL_GROUP = 128
NSA_HEADS = 8
NSA_KV_HEADS = 2
NSA_HPG = NSA_HEADS // NSA_KV_HEADS
NSA_HEAD_DIM = 64
CMP_BLOCK = 32
CMP_STRIDE = 16
SEL_BLOCK = 64
SEL_TOPK = 8
WINDOW = 256
GMLP_WIDTH = 512
GMLP_GROUPS = 4
GMLP_CHUNK = 128
POOL_WIDTH = 512
NSA_WIDTH = NSA_HEADS * NSA_HEAD_DIM

LANES = 128
VMEM_LIMIT_BYTES = 56 * 1024 * 1024

FFN_TM = 512
FFN_CHUNK = 512
PROJ_TM = 512
MERGE_TM = 512
MEM_TM = 512
NSA_TQ = 512
WIN_TQ = 128


def _dot(a, b):
    return jnp.dot(a, b, preferred_element_type=F32)


def _dot_nt(a, b):
    return lax.dot_general(a, b, (((1,), (1,)), ((), ())), preferred_element_type=F32)


def _rms(x, g):
    return x * lax.rsqrt(jnp.mean(x * x, axis=-1, keepdims=True) + EPS) * g


def _gelu(x):
    c = np.float32(np.sqrt(2.0 / np.pi))
    return x * (0.5 * (1.0 + jnp.tanh(c * (x + 0.044715 * (x * x * x)))))


def _const_spec(shape, layer=None):
    n = len(shape)
    if layer is None:
        return pl.BlockSpec(shape, lambda *_: (0,) * n, pipeline_mode=pl.Buffered(1))
    return pl.BlockSpec((None,) + tuple(shape), lambda *_: (layer,) + (0,) * n, pipeline_mode=pl.Buffered(1))


def _params(*sem):
    return pltpu.CompilerParams(dimension_semantics=sem, vmem_limit_bytes=VMEM_LIMIT_BYTES)


def _ffn_kernel(x_ref, gpre_ref, w1_ref, w3_ref, w2_ref, gpost_ref, o_ref, h_ref, acc_ref, *, chunks):
    x = x_ref[...]
    h_ref[...] = _rms(x, gpre_ref[...]).astype(BF16)
    for n, (off, width) in enumerate(chunks):
        h = h_ref[...]
        a = _dot(h, w1_ref[:, off:off + width])
        b = _dot(h, w3_ref[:, off:off + width])
        u = (a * jax.nn.sigmoid(a) * b).astype(BF16)
        y = _dot(u, w2_ref[off:off + width, :])
        if n == 0:
            acc_ref[...] = y
        else:
            acc_ref[...] += y
    o_ref[...] = x + 0.5 * _rms(acc_ref[...], gpost_ref[...])


def _ffn(x, l, gpre, w1, w3, w2, gpost):
    t, d = x.shape
    f = w1.shape[-1]
    tm = FFN_TM
    cs = functools.partial(_const_spec, layer=l)
    chunks = tuple((off, min(FFN_CHUNK, f - off)) for off in range(0, f, FFN_CHUNK))
    row = pl.BlockSpec((tm, d), lambda i: (i, 0))
    return pl.pallas_call(
        functools.partial(_ffn_kernel, chunks=chunks),
        out_shape=jax.ShapeDtypeStruct((t, d), F32),
        grid=(t // tm,),
        in_specs=[row, cs((1, d)), cs((d, f)), cs((d, f)), cs((f, d)), cs((1, d))],
        out_specs=row,
        scratch_shapes=[pltpu.VMEM((tm, d), BF16), pltpu.VMEM((tm, d), F32)],
        compiler_params=_params("parallel"),
        name="ffn",
    )(x, gpre, w1, w3, w2, gpost)


_A0, _Q0, _KV0, _PROJ_COLS = 0, 512, 1024, 1792


def _inproj_kernel(x_ref, g_ref, w_ref, b_ref, wg_ref, bg_ref, wgm_ref, bgm_ref, ws_ref, lng_ref, lnb_ref, bsx_ref,
                   a_ref, q_ref, kvc_ref, kv4_ref, gate_ref, gm_ref, *, tm):
    h = _rms(x_ref[...], g_ref[...]).astype(BF16)
    z = _dot(h, w_ref[...]) + b_ref[...]
    a_ref[...] = z[:, _A0:_A0 + POOL_WIDTH]
    zq = z[:, _Q0:_Q0 + NSA_WIDTH] * (NSA_HEAD_DIM ** -0.5 * LOG2E)
    for hh in range(NSA_HEADS):
        q_ref[hh] = zq[:, hh * 64:(hh + 1) * 64].astype(BF16)
    kvc_ref[0] = z[:, _KV0:_KV0 + LANES]
    kvc_ref[1] = z[:, _KV0 + LANES:_KV0 + 2 * LANES]
    for r in range(8):
        c0 = _KV0 + 256 + r * 64
        kv4_ref[r] = z[:, c0:c0 + 64].astype(BF16)
    gate_ref[...] = jax.nn.sigmoid(_dot(h, wg_ref[...]) + bg_ref[...])
    gm = _gelu(_dot(h, wgm_ref[...]) + bgm_ref[...])
    u = gm[:, :GMLP_WIDTH]
    v = gm[:, GMLP_WIDTH:]
    mu = jnp.mean(v, axis=-1, keepdims=True)
    vc = v - mu
    var = jnp.mean(vc * vc, axis=-1, keepdims=True)
    vn = (vc * lax.rsqrt(var + EPS) * lng_ref[...] + lnb_ref[...]).astype(BF16)
    ri = lax.broadcasted_iota(jnp.int32, (GMLP_CHUNK, GMLP_CHUNK), 0)
    ci = lax.broadcasted_iota(jnp.int32, (GMLP_CHUNK, GMLP_CHUNK), 1)
    for g in range(GMLP_GROUPS):
        wg = jnp.where(ri >= ci, ws_ref[g], 0.0).astype(BF16)
        cs = slice(g * 128, (g + 1) * 128)
        for c in range(tm // GMLP_CHUNK):
            rs = slice(c * GMLP_CHUNK, (c + 1) * GMLP_CHUNK)
            s = _dot(wg, vn[rs, cs]) + bsx_ref[:, cs]
            gm_ref[rs, cs] = (u[rs, cs] * s).astype(BF16)


def _inproj(x, l, g, w, b, wg, bg, wgm, bgm, ws, lng, lnb, bsx):
    t, d = x.shape
    tm = PROJ_TM
    cs = functools.partial(_const_spec, layer=l)
    row = lambda width: pl.BlockSpec((tm, width), lambda i: (i, 0))
    heads = pl.BlockSpec((8, tm, 64), lambda i: (0, i, 0))
    return pl.pallas_call(
        functools.partial(_inproj_kernel, tm=tm),
        out_shape=(
            jax.ShapeDtypeStruct((t, POOL_WIDTH), F32),
            jax.ShapeDtypeStruct((8, t, 64), BF16),
            jax.ShapeDtypeStruct((2, t, LANES), F32),
            jax.ShapeDtypeStruct((8, t, 64), BF16),
            jax.ShapeDtypeStruct((t, LANES), F32),
            jax.ShapeDtypeStruct((t, GMLP_WIDTH), BF16),
        ),
        grid=(t // tm,),
        in_specs=[row(d), cs((1, d)), cs((d, _PROJ_COLS)), cs((1, _PROJ_COLS)), cs((d, LANES)), cs((1, LANES)),
                  cs((d, 2 * GMLP_WIDTH)), cs((1, 2 * GMLP_WIDTH)),
                  cs((GMLP_GROUPS, GMLP_CHUNK, GMLP_CHUNK)), cs((1, GMLP_WIDTH)), cs((1, GMLP_WIDTH)),
                  cs((GMLP_CHUNK, GMLP_WIDTH))],
        out_specs=(row(POOL_WIDTH), heads, pl.BlockSpec((2, tm, LANES), lambda i: (0, i, 0)), heads, row(LANES),
                   row(GMLP_WIDTH)),
        compiler_params=_params("parallel"),
        name="inproj",
    )(x, g, w, b, wg, bg, wgm, bgm, ws, lng, lnb, bsx)


_POOL_PAD = 16


def _pool_kernel(a_ref, pw_ref, ps_ref, o_ref, pad_ref, *, seq):
    pos = lax.broadcasted_iota(jnp.int32, (seq, 1), 0).astype(F32) + 1.0
    pad_ref[0:_POOL_PAD, :] = jnp.zeros((_POOL_PAD, POOL_GROUP), F32)
    for gi, w in enumerate(POOL_WINDOWS):
        cs = slice(gi * POOL_GROUP, (gi + 1) * POOL_GROUP)
        x = a_ref[:, cs]
        s = x
        k = 1
        while k < w:
            pad_ref[_POOL_PAD:_POOL_PAD + seq, :] = s
            s = s + pad_ref[_POOL_PAD - k:_POOL_PAD - k + seq, :]
            k *= 2
        pooled = (s / jnp.minimum(pos, float(w)) - x).astype(BF16)
        o_ref[:, cs] = (_dot(pooled, pw_ref[gi]) * ps_ref[:, cs]).astype(BF16)


def _pool(a, l, pw, ps, batch, seq):
    t = a.shape[0]
    blk = pl.BlockSpec((seq, POOL_WIDTH), lambda b: (b, 0))
    cs = functools.partial(_const_spec, layer=l)
    return pl.pallas_call(
        functools.partial(_pool_kernel, seq=seq),
        out_shape=jax.ShapeDtypeStruct((t, POOL_WIDTH), BF16),
        grid=(batch,),
        in_specs=[blk, cs((len(POOL_WINDOWS), POOL_GROUP, POOL_GROUP)), cs((1, POOL_WIDTH))],
        out_specs=blk,
        scratch_shapes=[pltpu.VMEM((_POOL_PAD + seq, POOL_GROUP), F32)],
        compiler_params=_params("parallel"),
        name="pool",
    )(a, pw, ps)


def _cmpkv_kernel(x_ref, pos_ref, w1_ref, w2_ref, o_ref):
    nrow = x_ref.shape[0] // CMP_STRIDE
    p = q = None
    for l in range(CMP_STRIDE):
        xl = x_ref[pl.ds(l, nrow, stride=CMP_STRIDE), :]
        pl_ = _dot((xl + pos_ref[l:l + 1, :]).astype(BF16), w1_ref[l])
        ql = _dot((xl + pos_ref[CMP_STRIDE + l:CMP_STRIDE + l + 1, :]).astype(BF16), w1_ref[CMP_STRIDE + l])
        p = pl_ if p is None else p + pl_
        q = ql if q is None else q + ql
    hpre = p + pltpu.roll(q, nrow - 1, axis=0)
    h = _gelu(hpre).astype(BF16)
    hid = w2_ref.shape[0]
    for g in range(NSA_KV_HEADS):
        o_ref[g] = _dot(h[:, g * hid:(g + 1) * hid], w2_ref[...]).astype(BF16)


def _cmpkv(kvc, l, pos, w1, w2, batch, seq):
    nrow = seq // CMP_STRIDE
    hid = w2.shape[-2]
    return pl.pallas_call(
        _cmpkv_kernel,
        out_shape=jax.ShapeDtypeStruct((batch, 2 * NSA_KV_HEADS, nrow, NSA_HEAD_DIM), BF16),
        grid=(2, batch),
        in_specs=[pl.BlockSpec((None, seq, LANES), lambda r, b: (r, b, 0)),
                  pl.BlockSpec((None, None, CMP_BLOCK, LANES), lambda r, b: (l, r, 0, 0)),
                  pl.BlockSpec((None, None, CMP_BLOCK, LANES, NSA_KV_HEADS * hid), lambda r, b: (l, r, 0, 0, 0)),
                  pl.BlockSpec((None, None, hid, NSA_HEAD_DIM), lambda r, b: (l, r, 0, 0))],
        out_specs=pl.BlockSpec((None, NSA_KV_HEADS, nrow, NSA_HEAD_DIM), lambda r, b: (b, r, 0, 0)),
        compiler_params=_params("arbitrary", "arbitrary"),
        name="cmpkv",
    )(kvc, pos, w1, w2)


def _transpose_bf16(eye, x):
    return _dot_nt(eye, x).astype(BF16)


def _nsa_kernel(q_ref, cmp_ref, kv_ref, gate_ref, ovt_ref, extn_ref, o_ref, kaug_ref, vst_ref, vwt_ref, *, tq):
    i = pl.program_id(1)
    tk = tq
    hd = NSA_HEAD_DIM
    m4 = NSA_HPG * tq
    seq = kv_ref.shape[1]
    n_sel = seq // SEL_BLOCK
    groups = range(NSA_KV_HEADS)
    eye = (lax.broadcasted_iota(jnp.int32, (hd, hd), 0)
           == lax.broadcasted_iota(jnp.int32, (hd, hd), 1)).astype(BF16)

    @pl.when(i == 0)
    def _():
        place = (lax.broadcasted_iota(jnp.int32, (hd, LANES), 0)
                 == lax.broadcasted_iota(jnp.int32, (hd, LANES), 1)).astype(BF16)
        for g in groups:
            kaug_ref[g, :, 0:LANES] = extn_ref[...]
            for kt in range(seq // tk):
                rs = slice(kt * tk, (kt + 1) * tk)
                kaug_ref[g, rs, LANES:2 * LANES] = _dot(kv_ref[g, rs, :], place).astype(BF16)
                vst_ref[g, :, rs] = _transpose_bf16(eye, kv_ref[2 + g, rs, :])
                vwt_ref[g, :, rs] = _transpose_bf16(eye, kv_ref[6 + g, rs, :])

    col = lax.broadcasted_iota(jnp.int32, (1, m4), 1)
    t4 = i * tq + jnp.bitwise_and(col, tq - 1)
    t1 = i * tq + lax.broadcasted_iota(jnp.int32, (1, tq), 1)
    krow = lax.broadcasted_iota(jnp.int32, (tk, 1), 0)
    m0 = jnp.full((1, m4), NEG, F32)
    l0 = jnp.zeros((1, m4), F32)
    acc0 = jnp.zeros((hd, m4), F32)

    def attend(s, vt_tile, m_run, l_run, acc):
        m_new = jnp.maximum(m_run, jnp.max(s, axis=0, keepdims=True))
        alpha = jnp.exp2(m_run - m_new)
        e = jnp.exp2(s - m_new)
        l_new = alpha * l_run + jnp.sum(e, axis=0, keepdims=True)
        return m_new, l_new, alpha * acc + _dot(vt_tile, e.astype(BF16))

    qt = [jnp.concatenate([_transpose_bf16(eye, q_ref[NSA_HPG * g + h]) for h in range(NSA_HPG)], axis=1)
          for g in groups]

    tw = min(tq, WIN_TQ)
    n_w = WINDOW + tw
    win_bias, win_off = [], []
    for j in range(tq // tw):
        off = pl.multiple_of(jnp.maximum(i * tq + j * tw - WINDOW, 0), tw)
        kp = off + lax.broadcasted_iota(jnp.int32, (n_w, 1), 0)
        tj = t1[:, j * tw:(j + 1) * tw]
        bias = jnp.where((kp > tj - WINDOW) & (kp <= tj), 0.0, NEG)
        win_bias.append(jnp.concatenate([bias] * NSA_HPG, axis=1))
        win_off.append(off)

    def window(g):
        parts = []
        for j in range(tq // tw):
            cols = [qt[g][:, h * tq + j * tw:h * tq + (j + 1) * tw] for h in range(NSA_HPG)]
            s = _dot(kv_ref[4 + g, pl.ds(win_off[j], n_w), :], jnp.concatenate(cols, axis=1)) + win_bias[j]
            e = jnp.exp2(s - jnp.max(s, axis=0, keepdims=True))
            acc = _dot(vwt_ref[g, :, pl.ds(win_off[j], n_w)], e.astype(BF16))
            parts.append(acc / jnp.sum(e, axis=0, keepdims=True))
        return jnp.concatenate([p[:, h * tw:(h + 1) * tw] for h in range(NSA_HPG) for p in parts], axis=1)

    def compressed(g):
        ncmp = cmp_ref.shape[1]
        crow = lax.broadcasted_iota(jnp.int32, (ncmp, 1), 0)
        valid = (crow * CMP_STRIDE + (CMP_BLOCK - 1)) <= t4
        s = jnp.where(valid, _dot(cmp_ref[g], qt[g]), NEG)
        e = jnp.exp2(s - jnp.max(s, axis=0, keepdims=True))
        p = jnp.where(valid, e / jnp.sum(e, axis=0, keepdims=True), 0.0)
        o_cmp = _dot(_transpose_bf16(eye, cmp_ref[2 + g]), p.astype(BF16))
        psum = p[:, 0:tq] + p[:, tq:2 * tq] + p[:, 2 * tq:3 * tq] + p[:, 3 * tq:4 * tq]
        p_hi = psum.astype(BF16)
        p_lo = (psum - p_hi.astype(F32)).astype(BF16)
        imp = (_dot(ovt_ref[...], p_hi) + _dot(ovt_ref[...], p_lo))[0:n_sel, :]
        jrow = lax.broadcasted_iota(jnp.int32, (n_sel, 1), 0)
        cur = jnp.right_shift(t1, 6)
        forced = (jrow == 0) | (jrow == cur) | (jrow == cur - 1)
        vt = jnp.where(forced, 1e9, jnp.where(jrow * SEL_BLOCK <= t1, imp, -1.0))
        n_acc = 4
        cnt = [jnp.zeros((n_sel, tq), F32) for _ in range(n_acc)]
        for jp in range(n_sel):
            r = vt[jp:jp + 1, :]
            tie = jnp.where(jrow > jp, 1.0, 0.0)
            cnt[jp % n_acc] = cnt[jp % n_acc] + jnp.where(r > vt, 1.0, jnp.where(r == vt, tie, 0.0))
        rank = (cnt[0] + cnt[1]) + (cnt[2] + cnt[3])
        notsel = jnp.where(rank < float(SEL_TOPK), 0.0, 1.0)
        notsel = jnp.concatenate([notsel, jnp.zeros((LANES - n_sel, tq), F32)], axis=0).astype(BF16)
        qaug = jnp.concatenate([jnp.concatenate([notsel] * NSA_HPG, axis=1), qt[g], jnp.zeros((hd, m4), BF16)],
                               axis=0)
        return o_cmp, qaug

    o_win = [window(g) for g in groups]
    o_cmp, qaug = zip(*[compressed(g) for g in groups])

    def scores(g, kt):
        off = pl.multiple_of(kt * tk, tk)
        return _dot(kaug_ref[g, pl.ds(off, tk), :], qaug[g])

    def slc_body(kt, carry):
        off = pl.multiple_of(kt * tk, tk)
        return tuple(attend(scores(g, kt), vst_ref[g, :, pl.ds(off, tk)], *carry[g]) for g in groups)

    carry = lax.fori_loop(0, i, slc_body, tuple((m0, l0, acc0) for g in groups))
    causal = jnp.where((i * tk + krow) <= t1, 0.0, NEG)
    causal4 = jnp.concatenate([causal] * NSA_HPG, axis=1)
    off_d = pl.multiple_of(i * tk, tk)
    o_slc = []
    for g in groups:
        _, l_fin, acc_fin = attend(scores(g, i) + causal4, vst_ref[g, :, pl.ds(off_d, tk)], *carry[g])
        o_slc.append(acc_fin / l_fin)

    eye_q = (lax.broadcasted_iota(jnp.int32, (tq, tq), 0)
             == lax.broadcasted_iota(jnp.int32, (tq, tq), 1)).astype(BF16)
    o_heads = []
    gt = gate_ref[...].T
    for g in groups:
        for h in range(NSA_HPG):
            cs = slice(h * tq, (h + 1) * tq)
            r = 3 * (NSA_HPG * g + h)
            o = (gt[r:r + 1, :] * o_cmp[g][:, cs] + gt[r + 1:r + 2, :] * o_slc[g][:, cs]
                 + gt[r + 2:r + 3, :] * o_win[g][:, cs])
            o_heads.append(o.astype(BF16))
    o_ref[...] = _transpose_bf16(eye_q, jnp.concatenate(o_heads, axis=0))


def _nsa(q_hm, cmp, kv4, gates, ovt, extn, batch, seq):
    t = q_hm.shape[1]
    tq = NSA_TQ
    assert tq % min(tq, WIN_TQ) == 0 and WINDOW % min(tq, WIN_TQ) == 0 and WINDOW + tq <= seq
    nq = seq // tq
    ncmp = cmp.shape[2]
    qblk = pl.BlockSpec((NSA_HEADS, tq, NSA_HEAD_DIM), lambda b, i: (0, b * nq + i, 0))
    return pl.pallas_call(
        functools.partial(_nsa_kernel, tq=tq),
        out_shape=jax.ShapeDtypeStruct((t, NSA_WIDTH), BF16),
        grid=(batch, nq),
        in_specs=[qblk,
                  pl.BlockSpec((None, 4, ncmp, NSA_HEAD_DIM), lambda b, i: (b, 0, 0, 0)),
                  pl.BlockSpec((8, seq, NSA_HEAD_DIM), lambda b, i: (0, b, 0)),
                  pl.BlockSpec((tq, LANES), lambda b, i: (b * nq + i, 0)),
                  _const_spec(ovt.shape), _const_spec(extn.shape)],
        out_specs=pl.BlockSpec((tq, NSA_WIDTH), lambda b, i: (b * nq + i, 0)),
        scratch_shapes=[pltpu.VMEM((NSA_KV_HEADS, seq, 2 * LANES), BF16),
                        pltpu.VMEM((NSA_KV_HEADS, NSA_HEAD_DIM, seq), BF16),
                        pltpu.VMEM((NSA_KV_HEADS, NSA_HEAD_DIM, seq), BF16)],
        compiler_params=_params("arbitrary", "arbitrary"),
        name="nsa",
    )(q_hm, cmp, kv4, gates, ovt, extn)


def _merge_kernel(x_ref, pm_ref, on_ref, gm_ref, gpre_ref, wmg_ref, bmg_ref, wbp_ref, wbn_ref, wbg_ref,
                  wout_ref, gpost_ref, o_ref):
    x = x_ref[...]
    d = x.shape[1]
    h = _rms(x, gpre_ref[...]).astype(BF16)

    def gate(j):
        return jax.nn.sigmoid(_dot(h, wmg_ref[:, j * d:(j + 1) * d]) + bmg_ref[:, j * d:(j + 1) * d])

    comb = gate(0) * _dot(pm_ref[...], wbp_ref[...])
    comb = comb + gate(1) * _dot(on_ref[...], wbn_ref[...])
    comb = comb + gate(2) * _dot(gm_ref[...], wbg_ref[...])
    y = _dot(comb.astype(BF16), wout_ref[...])
    o_ref[...] = x + _rms(y, gpost_ref[...])


def _merge(x, l, pm, o_nsa, gm, gpre, wmg, bmg, wbp, wbn, wbg, wout, gpost):
    t, d = x.shape
    tm = MERGE_TM
    row = lambda width: pl.BlockSpec((tm, width), lambda i: (i, 0))
    cs = functools.partial(_const_spec, layer=l)
    return pl.pallas_call(
        _merge_kernel,
        out_shape=jax.ShapeDtypeStruct((t, d), F32),
        grid=(t // tm,),
        in_specs=[row(d), row(POOL_WIDTH), row(NSA_WIDTH), row(GMLP_WIDTH), cs((1, d)), cs((d, 3 * d)),
                  cs((1, 3 * d)), cs((POOL_WIDTH, d)), cs((NSA_WIDTH, d)), cs((GMLP_WIDTH, d)), cs((d, d)),
                  cs((1, d))],
        out_specs=row(d),
        compiler_params=_params("parallel"),
        name="merge",
    )(x, pm, o_nsa, gm, gpre, wmg, bmg, wbp, wbn, wbg, wout, gpost)


def _memkv_kernel(m_ref, g_ref, wk_ref, wv_ref, k_ref, v_ref):
    mn = _rms(m_ref[...], g_ref[...]).astype(BF16)
    k_ref[...] = _dot(mn, wk_ref[...]).astype(BF16)
    v_ref[...] = _dot(mn, wv_ref[...]).astype(BF16)


def _memkv(mem, l, g, wk, wv):
    batch, mlen, d = mem.shape
    blk = pl.BlockSpec((None, mlen, d), lambda b: (b, 0, 0))
    out = jax.ShapeDtypeStruct((batch, mlen, d), BF16)
    cs = functools.partial(_const_spec, layer=l)
    return pl.pallas_call(
        _memkv_kernel,
        out_shape=(out, out),
        grid=(batch,),
        in_specs=[blk, cs((1, d)), cs((d, d)), cs((d, d))],
        out_specs=(blk, blk),
        compiler_params=_params("parallel"),
        name="memkv",
    )(mem, g, wk, wv)


def _memattn_kernel(x_ref, gpre_ref, wq_ref, k_ref, v_ref, wo_ref, gpost_ref, o_ref):
    x = x_ref[...]
    d = x.shape[1]
    hd = d // MEM_HEADS
    h = _rms(x, gpre_ref[...]).astype(BF16)
    q = (_dot(h, wq_ref[...]) * (hd ** -0.5)).astype(BF16)
    outs = []
    for n in range(MEM_HEADS):
        cs = slice(n * hd, (n + 1) * hd)
        s = _dot_nt(q[:, cs], k_ref[:, cs])
        e = jnp.exp(s - jnp.max(s, axis=-1, keepdims=True))
        p = e / jnp.sum(e, axis=-1, keepdims=True)
        outs.append(_dot(p.astype(BF16), v_ref[:, cs]).astype(BF16))
    y = _dot(jnp.concatenate(outs, axis=1), wo_ref[...])
    o_ref[...] = x + _rms(y, gpost_ref[...])


def _memattn(x, l, gpre, wq, k, v, wo, gpost, batch, seq):
    t, d = x.shape
    mlen = k.shape[1]
    tm = MEM_TM
    nt = seq // tm
    row = pl.BlockSpec((tm, d), lambda b, i: (b * nt + i, 0))
    kvb = pl.BlockSpec((None, mlen, d), lambda b, i: (b, 0, 0))
    cs = functools.partial(_const_spec, layer=l)
    return pl.pallas_call(
        _memattn_kernel,
        out_shape=jax.ShapeDtypeStruct((t, d), F32),
        grid=(batch, nt),
        in_specs=[row, cs((1, d)), cs((d, d)), kvb, kvb, cs((d, d)), cs((1, d))],
        out_specs=row,
        compiler_params=_params("parallel", "arbitrary"),
        name="memattn",
    )(x, gpre, wq, k, v, wo, gpost)


def _overlap_matrix_t(n_cmp_pad, n_sel):
    cs = np.arange(n_cmp_pad)[None, :] * CMP_STRIDE
    ss = np.arange(LANES)[:, None] * SEL_BLOCK
    ov = np.minimum(cs + CMP_BLOCK, ss + SEL_BLOCK) - np.maximum(cs, ss)
    ov = np.maximum(ov, 0) / CMP_BLOCK
    ov[n_sel:, :] = 0.0
    return jnp.asarray(ov, dtype=BF16)


def _block_mask_matrix(seq):
    ex = (np.arange(seq)[:, None] // SEL_BLOCK) == np.arange(LANES)[None, :]
    return jnp.asarray(np.where(ex, -MASK_BIG, 0.0), dtype=BF16)


def kernel(x, mem, ff1_pre_g, ff1_w1, ff1_w3, ff1_w2, ff1_post_g, mix_pre_g, w_in, b_in, pool_w, pool_scale, cmp_pos_k, cmp_w1_k, cmp_w2_k, cmp_pos_v, cmp_w1_v, cmp_w2_v, gmlp_ln_g, gmlp_ln_b, gmlp_ws, gmlp_bs, w_br_pool, w_br_nsa, w_br_gmlp, w_mix_out, mix_post_g, mem_pre_g, mem_kv_g, mem_wq, mem_wk, mem_wv, mem_wo, mem_post_g, ff2_pre_g, ff2_w1, ff2_w3, ff2_w2, ff2_post_g):
    batch, seq, d = x.shape
    depth = w_in.shape[0]
    t = batch * seq
    n_cmp_rows = seq // CMP_STRIDE
    bf = lambda w: w.astype(BF16)
    vec = lambda g: g.reshape(depth, 1, -1)

    c_g = 1792
    c_gm = c_g + 3 * NSA_HEADS
    c_mg = c_gm + 2 * GMLP_WIDTH
    w_in_b = bf(w_in)
    n_gate = c_gm - c_g
    w_proj = w_in_b[:, :, :c_g]
    b_proj = b_in[:, :c_g].reshape(depth, 1, -1)
    w_gate = jnp.concatenate([w_in_b[:, :, c_g:c_gm], jnp.zeros((depth, d, LANES - n_gate), BF16)], axis=2)
    b_gate = jnp.concatenate([b_in[:, c_g:c_gm], jnp.zeros((depth, LANES - n_gate), b_in.dtype)],
                             axis=1).reshape(depth, 1, -1)
    w_gm = w_in_b[:, :, c_gm:c_mg]
    b_gm = b_in[:, c_gm:c_mg].reshape(depth, 1, -1)
    w_mg = w_in_b[:, :, c_mg:]
    b_mg = b_in[:, c_mg:].reshape(depth, 1, -1)
    bsx = jnp.broadcast_to(jnp.swapaxes(gmlp_bs, 1, 2)[:, :, :, None],
                           (depth, GMLP_CHUNK, GMLP_GROUPS, GMLP_WIDTH // GMLP_GROUPS)).reshape(
                               depth, GMLP_CHUNK, GMLP_WIDTH)
    cmp_pos = jnp.stack([cmp_pos_k, cmp_pos_v], axis=1)
    cmp_pos = jnp.concatenate([cmp_pos, cmp_pos], axis=-1)
    cmp_w1 = bf(jnp.stack([cmp_w1_k, cmp_w1_v], axis=1))
    w1_zero = jnp.zeros_like(cmp_w1)
    cmp_w1 = jnp.concatenate([jnp.concatenate([cmp_w1, w1_zero], axis=-1),
                              jnp.concatenate([w1_zero, cmp_w1], axis=-1)], axis=-2)
    cmp_w2 = bf(jnp.stack([cmp_w2_k, cmp_w2_v], axis=1))
    w_br_nsa_b = bf(w_br_nsa)
    ff1 = (vec(ff1_pre_g), bf(ff1_w1), bf(ff1_w3), bf(ff1_w2), vec(ff1_post_g))
    ff2 = (vec(ff2_pre_g), bf(ff2_w1), bf(ff2_w3), bf(ff2_w2), vec(ff2_post_g))
    pool_w_b, pool_scale_v = bf(pool_w), vec(pool_scale)
    w_br_pool_b, w_br_gmlp_b, w_mix_out_b = bf(w_br_pool), bf(w_br_gmlp), bf(w_mix_out)
    mem_wq_b, mem_wk_b, mem_wv_b, mem_wo_b = bf(mem_wq), bf(mem_wk), bf(mem_wv), bf(mem_wo)
    mix_pre, mix_post = vec(mix_pre_g), vec(mix_post_g)
    mem_pre, mem_kv, mem_post = vec(mem_pre_g), vec(mem_kv_g), vec(mem_post_g)
    ln_g, ln_b = vec(gmlp_ln_g), vec(gmlp_ln_b)
    ov = _overlap_matrix_t(n_cmp_rows, seq // SEL_BLOCK)
    ex = _block_mask_matrix(seq)

    xs = x.reshape(t, d)
    for l in range(depth):
        xs = _ffn(xs, l, *ff1)
        a, q_hm, kvc, kv4, gates, gm = _inproj(xs, l, mix_pre, w_proj, b_proj, w_gate, b_gate, w_gm, b_gm,
                                               gmlp_ws, ln_g, ln_b, bsx)
        pm = _pool(a, l, pool_w_b, pool_scale_v, batch, seq)
        cmp = _cmpkv(kvc, l, cmp_pos, cmp_w1, cmp_w2, batch, seq)
        o_nsa = _nsa(q_hm, cmp, kv4, gates, ov, ex, batch, seq)
        xs = _merge(xs, l, pm, o_nsa, gm, mix_pre, w_mg, b_mg, w_br_pool_b, w_br_nsa_b, w_br_gmlp_b,
                    w_mix_out_b, mix_post)
        mk, mv = _memkv(mem, l, mem_kv, mem_wk_b, mem_wv_b)
        xs = _memattn(xs, l, mem_pre, mem_wq_b, mk, mv, mem_wo_b, mem_post, batch, seq)
        xs = _ffn(xs, l, *ff2)
    return xs.reshape(batch, seq, d)
```

```python
import functools

import numpy as np
import jax
import jax.numpy as jnp
from jax import lax
from jax.experimental import pallas as pl
from jax.experimental.pallas import tpu as pltpu

F32 = jnp.float32
BF16 = jnp.bfloat16

EPS = 1e-6
NEG = -1e30
MASK_BIG = 2.0 ** 100
LOG2E = float(np.log2(np.e))

MEM_HEADS = 4
POOL_WINDOWS = (2, 4, 8, 16)
POOL_GROUP = 128
NSA_HEADS = 8
NSA_KV_HEADS = 2
NSA_HPG = NSA_HEADS // NSA_KV_HEADS
NSA_HEAD_DIM = 64
CMP_BLOCK = 32
CMP_STRIDE = 16
SEL_BLOCK = 64
SEL_TOPK = 8
WINDOW = 256
GMLP_WIDTH = 512
GMLP_GROUPS = 4
GMLP_CHUNK = 128
POOL_WIDTH = 512
NSA_WIDTH = NSA_HEADS * NSA_HEAD_DIM

LANES = 128
VMEM_LIMIT_BYTES = 56 * 1024 * 1024

FFN_TM = 512
FFN_CHUNK = 512
PROJ_TM = 512
MERGE_TM = 512
MEM_TM = 1024
NSA_TQ = 512
WIN_TQ = 128


def _dot(a, b):
    return jnp.dot(a, b, preferred_element_type=F32)


def _dot_nt(a, b):
    return lax.dot_general(a, b, (((1,), (1,)), ((), ())), preferred_element_type=F32)


def _rms(x, g):
    return x * lax.rsqrt(jnp.mean(x * x, axis=-1, keepdims=True) + EPS) * g


def _gelu(x):
    c = np.float32(np.sqrt(2.0 / np.pi))
    return x * (0.5 * (1.0 + jnp.tanh(c * (x + 0.044715 * (x * x * x)))))


def _const_spec(shape, layer=None):
    n = len(shape)
    if layer is None:
        return pl.BlockSpec(shape, lambda *_: (0,) * n, pipeline_mode=pl.Buffered(1))
    return pl.BlockSpec((None,) + tuple(shape), lambda *_: (layer,) + (0,) * n, pipeline_mode=pl.Buffered(1))


def _params(*sem):
    return pltpu.CompilerParams(dimension_semantics=sem, vmem_limit_bytes=VMEM_LIMIT_BYTES)


def _ffn_kernel(x_ref, gpre_ref, w1_ref, w3_ref, w2_ref, gpost_ref, o_ref, h_ref, acc_ref, *, chunks):
    x = x_ref[...]
    h_ref[...] = _rms(x, gpre_ref[...]).astype(BF16)
    for n, (off, width) in enumerate(chunks):
        h = h_ref[...]
        a = _dot(h, w1_ref[:, off:off + width])
        b = _dot(h, w3_ref[:, off:off + width])
        u = (a * jax.nn.sigmoid(a) * b).astype(BF16)
        y = _dot(u, w2_ref[off:off + width, :])
        if n == 0:
            acc_ref[...] = y
        else:
            acc_ref[...] += y
    o_ref[...] = x + 0.5 * _rms(acc_ref[...], gpost_ref[...])


def _ffn(x, l, gpre, w1, w3, w2, gpost):
    t, d = x.shape
    f = w1.shape[-1]
    tm = FFN_TM
    cs = functools.partial(_const_spec, layer=l)
    chunks = tuple((off, min(FFN_CHUNK, f - off)) for off in range(0, f, FFN_CHUNK))
    row = pl.BlockSpec((tm, d), lambda i: (i, 0))
    return pl.pallas_call(
        functools.partial(_ffn_kernel, chunks=chunks),
        out_shape=jax.ShapeDtypeStruct((t, d), F32),
        grid=(t // tm,),
        in_specs=[row, cs((1, d)), cs((d, f)), cs((d, f)), cs((f, d)), cs((1, d))],
        out_specs=row,
        scratch_shapes=[pltpu.VMEM((tm, d), BF16), pltpu.VMEM((tm, d), F32)],
        compiler_params=_params("parallel"),
        name="ffn",
    )(x, gpre, w1, w3, w2, gpost)


_A0, _Q0, _KV0, _PROJ_COLS = 0, 512, 1024, 1792


def _inproj_kernel(x_ref, g_ref, w_ref, b_ref, wg_ref, bg_ref, wgm_ref, bgm_ref, ws_ref, lng_ref, lnb_ref, bsx_ref,
                   a_ref, q_ref, kvc_ref, kv4_ref, gate_ref, gm_ref, *, tm):
    h = _rms(x_ref[...], g_ref[...]).astype(BF16)
    z = _dot(h, w_ref[...]) + b_ref[...]
    a_ref[...] = z[:, _A0:_A0 + POOL_WIDTH]
    zq = z[:, _Q0:_Q0 + NSA_WIDTH] * (NSA_HEAD_DIM ** -0.5 * LOG2E)
    for hh in range(NSA_HEADS):
        q_ref[hh] = zq[:, hh * 64:(hh + 1) * 64].astype(BF16)
    kvc_ref[0] = z[:, _KV0:_KV0 + LANES]
    kvc_ref[1] = z[:, _KV0 + LANES:_KV0 + 2 * LANES]
    for r in range(8):
        c0 = _KV0 + 256 + r * 64
        kv4_ref[r] = z[:, c0:c0 + 64].astype(BF16)
    gate_ref[...] = jax.nn.sigmoid(_dot(h, wg_ref[...]) + bg_ref[...])
    gm = _gelu(_dot(h, wgm_ref[...]) + bgm_ref[...])
    u = gm[:, :GMLP_WIDTH]
    v = gm[:, GMLP_WIDTH:]
    mu = jnp.mean(v, axis=-1, keepdims=True)
    vc = v - mu
    var = jnp.mean(vc * vc, axis=-1, keepdims=True)
    vn = (vc * lax.rsqrt(var + EPS) * lng_ref[...] + lnb_ref[...]).astype(BF16)
    ri = lax.broadcasted_iota(jnp.int32, (GMLP_CHUNK, GMLP_CHUNK), 0)
    ci = lax.broadcasted_iota(jnp.int32, (GMLP_CHUNK, GMLP_CHUNK), 1)
    for g in range(GMLP_GROUPS):
        wg = jnp.where(ri >= ci, ws_ref[g], 0.0).astype(BF16)
        cs = slice(g * 128, (g + 1) * 128)
        for c in range(tm // GMLP_CHUNK):
            rs = slice(c * GMLP_CHUNK, (c + 1) * GMLP_CHUNK)
            s = _dot(wg, vn[rs, cs]) + bsx_ref[:, cs]
            gm_ref[rs, cs] = (u[rs, cs] * s).astype(BF16)


def _inproj(x, l, g, w, b, wg, bg, wgm, bgm, ws, lng, lnb, bsx):
    t, d = x.shape
    tm = PROJ_TM
    cs = functools.partial(_const_spec, layer=l)
    row = lambda width: pl.BlockSpec((tm, width), lambda i: (i, 0))
    heads = pl.BlockSpec((8, tm, 64), lambda i: (0, i, 0))
    return pl.pallas_call(
        functools.partial(_inproj_kernel, tm=tm),
        out_shape=(
            jax.ShapeDtypeStruct((t, POOL_WIDTH), F32),
            jax.ShapeDtypeStruct((8, t, 64), BF16),
            jax.ShapeDtypeStruct((2, t, LANES), F32),
            jax.ShapeDtypeStruct((8, t, 64), BF16),
            jax.ShapeDtypeStruct((t, LANES), F32),
            jax.ShapeDtypeStruct((t, GMLP_WIDTH), BF16),
        ),
        grid=(t // tm,),
        in_specs=[row(d), cs((1, d)), cs((d, _PROJ_COLS)), cs((1, _PROJ_COLS)), cs((d, LANES)), cs((1, LANES)),
                  cs((d, 2 * GMLP_WIDTH)), cs((1, 2 * GMLP_WIDTH)),
                  cs((GMLP_GROUPS, GMLP_CHUNK, GMLP_CHUNK)), cs((1, GMLP_WIDTH)), cs((1, GMLP_WIDTH)),
                  cs((GMLP_CHUNK, GMLP_WIDTH))],
        out_specs=(row(POOL_WIDTH), heads, pl.BlockSpec((2, tm, LANES), lambda i: (0, i, 0)), heads, row(LANES),
                   row(GMLP_WIDTH)),
        compiler_params=_params("parallel"),
        name="inproj",
    )(x, g, w, b, wg, bg, wgm, bgm, ws, lng, lnb, bsx)


_POOL_PAD = 16


def _pool_kernel(a_ref, pw_ref, ps_ref, o_ref, pad_ref, *, seq):
    pos = lax.broadcasted_iota(jnp.int32, (seq, 1), 0).astype(F32) + 1.0
    pad_ref[0:_POOL_PAD, :] = jnp.zeros((_POOL_PAD, POOL_GROUP), F32)
    for gi, w in enumerate(POOL_WINDOWS):
        cs = slice(gi * POOL_GROUP, (gi + 1) * POOL_GROUP)
        x = a_ref[:, cs]
        s = x
        k = 1
        while k < w:
            pad_ref[_POOL_PAD:_POOL_PAD + seq, :] = s
            s = s + pad_ref[_POOL_PAD - k:_POOL_PAD - k + seq, :]
            k *= 2
        pooled = (s / jnp.minimum(pos, float(w)) - x).astype(BF16)
        o_ref[:, cs] = (_dot(pooled, pw_ref[gi]) * ps_ref[:, cs]).astype(BF16)


def _pool(a, l, pw, ps, batch, seq):
    t = a.shape[0]
    blk = pl.BlockSpec((seq, POOL_WIDTH), lambda b: (b, 0))
    cs = functools.partial(_const_spec, layer=l)
    return pl.pallas_call(
        functools.partial(_pool_kernel, seq=seq),
        out_shape=jax.ShapeDtypeStruct((t, POOL_WIDTH), BF16),
        grid=(batch,),
        in_specs=[blk, cs((len(POOL_WINDOWS), POOL_GROUP, POOL_GROUP)), cs((1, POOL_WIDTH))],
        out_specs=blk,
        scratch_shapes=[pltpu.VMEM((_POOL_PAD + seq, POOL_GROUP), F32)],
        compiler_params=_params("parallel"),
        name="pool",
    )(a, pw, ps)


def _cmpkv_kernel(x_ref, pos_ref, w1_ref, w2_ref, o_ref):
    nrow = x_ref.shape[0] // CMP_STRIDE
    p = q = None
    for l in range(CMP_STRIDE):
        xl = x_ref[pl.ds(l, nrow, stride=CMP_STRIDE), :]
        pl_ = _dot((xl + pos_ref[l:l + 1, :]).astype(BF16), w1_ref[l])
        ql = _dot((xl + pos_ref[CMP_STRIDE + l:CMP_STRIDE + l + 1, :]).astype(BF16), w1_ref[CMP_STRIDE + l])
        p = pl_ if p is None else p + pl_
        q = ql if q is None else q + ql
    hpre = p + pltpu.roll(q, nrow - 1, axis=0)
    h = _gelu(hpre).astype(BF16)
    hid = w2_ref.shape[0]
    for g in range(NSA_KV_HEADS):
        o_ref[g] = _dot(h[:, g * hid:(g + 1) * hid], w2_ref[...]).astype(BF16)


def _cmpkv(kvc, l, pos, w1, w2, batch, seq):
    nrow = seq // CMP_STRIDE
    hid = w2.shape[-2]
    return pl.pallas_call(
        _cmpkv_kernel,
        out_shape=jax.ShapeDtypeStruct((batch, 2 * NSA_KV_HEADS, nrow, NSA_HEAD_DIM), BF16),
        grid=(2, batch),
        in_specs=[pl.BlockSpec((None, seq, LANES), lambda r, b: (r, b, 0)),
                  pl.BlockSpec((None, None, CMP_BLOCK, LANES), lambda r, b: (l, r, 0, 0)),
                  pl.BlockSpec((None, None, CMP_BLOCK, LANES, NSA_KV_HEADS * hid), lambda r, b: (l, r, 0, 0, 0)),
                  pl.BlockSpec((None, None, hid, NSA_HEAD_DIM), lambda r, b: (l, r, 0, 0))],
        out_specs=pl.BlockSpec((None, NSA_KV_HEADS, nrow, NSA_HEAD_DIM), lambda r, b: (b, r, 0, 0)),
        compiler_params=_params("arbitrary", "arbitrary"),
        name="cmpkv",
    )(kvc, pos, w1, w2)


def _transpose_bf16(eye, x):
    return _dot_nt(eye, x).astype(BF16)


def _nsa_kernel(q_ref, cmp_ref, kv_ref, gate_ref, ovt_ref, extn_ref, o_ref, kaug_ref, vst_ref, vwt_ref, *, tq):
    i = pl.program_id(1)
    tk = tq
    hd = NSA_HEAD_DIM
    m4 = NSA_HPG * tq
    seq = kv_ref.shape[1]
    n_sel = seq // SEL_BLOCK
    groups = range(NSA_KV_HEADS)
    eye = (lax.broadcasted_iota(jnp.int32, (hd, hd), 0)
           == lax.broadcasted_iota(jnp.int32, (hd, hd), 1)).astype(BF16)

    @pl.when(i == 0)
    def _():
        place = (lax.broadcasted_iota(jnp.int32, (hd, LANES), 0)
                 == lax.broadcasted_iota(jnp.int32, (hd, LANES), 1)).astype(BF16)
        for g in groups:
            kaug_ref[g, :, 0:LANES] = extn_ref[...]
            for kt in range(seq // tk):
                rs = slice(kt * tk, (kt + 1) * tk)
                kaug_ref[g, rs, LANES:2 * LANES] = _dot(kv_ref[g, rs, :], place).astype(BF16)
                vst_ref[g, :, rs] = _transpose_bf16(eye, kv_ref[2 + g, rs, :])
                vwt_ref[g, :, rs] = _transpose_bf16(eye, kv_ref[6 + g, rs, :])

    col = lax.broadcasted_iota(jnp.int32, (1, m4), 1)
    t4 = i * tq + jnp.bitwise_and(col, tq - 1)
    t1 = i * tq + lax.broadcasted_iota(jnp.int32, (1, tq), 1)
    krow = lax.broadcasted_iota(jnp.int32, (tk, 1), 0)
    m0 = jnp.full((1, m4), NEG, F32)
    l0 = jnp.zeros((1, m4), F32)
    acc0 = jnp.zeros((hd, m4), F32)

    def attend(s, vt_tile, m_run, l_run, acc):
        m_new = jnp.maximum(m_run, jnp.max(s, axis=0, keepdims=True))
        alpha = jnp.exp2(m_run - m_new)
        e = jnp.exp2(s - m_new)
        l_new = alpha * l_run + jnp.sum(e, axis=0, keepdims=True)
        return m_new, l_new, alpha * acc + _dot(vt_tile, e.astype(BF16))

    qt = [jnp.concatenate([_transpose_bf16(eye, q_ref[NSA_HPG * g + h]) for h in range(NSA_HPG)], axis=1)
          for g in groups]

    tw = min(tq, WIN_TQ)
    n_w = WINDOW + tw
    win_bias, win_off = [], []
    for j in range(tq // tw):
        off = pl.multiple_of(jnp.maximum(i * tq + j * tw - WINDOW, 0), tw)
        kp = off + lax.broadcasted_iota(jnp.int32, (n_w, 1), 0)
        tj = t1[:, j * tw:(j + 1) * tw]
        bias = jnp.where((kp > tj - WINDOW) & (kp <= tj), 0.0, NEG)
        win_bias.append(jnp.concatenate([bias] * NSA_HPG, axis=1))
        win_off.append(off)

    def window(g):
        parts = []
        for j in range(tq // tw):
            cols = [qt[g][:, h * tq + j * tw:h * tq + (j + 1) * tw] for h in range(NSA_HPG)]
            s = _dot(kv_ref[4 + g, pl.ds(win_off[j], n_w), :], jnp.concatenate(cols, axis=1)) + win_bias[j]
            e = jnp.exp2(s - jnp.max(s, axis=0, keepdims=True))
            acc = _dot(vwt_ref[g, :, pl.ds(win_off[j], n_w)], e.astype(BF16))
            parts.append(acc / jnp.sum(e, axis=0, keepdims=True))
        return jnp.concatenate([p[:, h * tw:(h + 1) * tw] for h in range(NSA_HPG) for p in parts], axis=1)

    def compressed(g):
        ncmp = cmp_ref.shape[1]
        crow = lax.broadcasted_iota(jnp.int32, (ncmp, 1), 0)
        valid = (crow * CMP_STRIDE + (CMP_BLOCK - 1)) <= t4
        s = jnp.where(valid, _dot(cmp_ref[g], qt[g]), NEG)
        e = jnp.exp2(s - jnp.max(s, axis=0, keepdims=True))
        p = jnp.where(valid, e / jnp.sum(e, axis=0, keepdims=True), 0.0)
        o_cmp = _dot(_transpose_bf16(eye, cmp_ref[2 + g]), p.astype(BF16))
        psum = p[:, 0:tq] + p[:, tq:2 * tq] + p[:, 2 * tq:3 * tq] + p[:, 3 * tq:4 * tq]
        p_hi = psum.astype(BF16)
        p_lo = (psum - p_hi.astype(F32)).astype(BF16)
        imp = (_dot(ovt_ref[...], p_hi) + _dot(ovt_ref[...], p_lo))[0:n_sel, :]
        jrow = lax.broadcasted_iota(jnp.int32, (n_sel, 1), 0)
        cur = jnp.right_shift(t1, 6)
        forced = (jrow == 0) | (jrow == cur) | (jrow == cur - 1)
        vt = jnp.where(forced, 1e9, jnp.where(jrow * SEL_BLOCK <= t1, imp, -1.0))
        n_acc = 4
        cnt = [jnp.zeros((n_sel, tq), F32) for _ in range(n_acc)]
        for jp in range(n_sel):
            r = vt[jp:jp + 1, :]
            tie = jnp.where(jrow > jp, 1.0, 0.0)
            cnt[jp % n_acc] = cnt[jp % n_acc] + jnp.where(r > vt, 1.0, jnp.where(r == vt, tie, 0.0))
        rank = (cnt[0] + cnt[1]) + (cnt[2] + cnt[3])
        notsel = jnp.where(rank < float(SEL_TOPK), 0.0, 1.0)
        notsel = jnp.concatenate([notsel, jnp.zeros((LANES - n_sel, tq), F32)], axis=0).astype(BF16)
        qaug = jnp.concatenate([jnp.concatenate([notsel] * NSA_HPG, axis=1), qt[g], jnp.zeros((hd, m4), BF16)],
                               axis=0)
        return o_cmp, qaug

    o_win = [window(g) for g in groups]
    o_cmp, qaug = zip(*[compressed(g) for g in groups])

    def scores(g, kt):
        off = pl.multiple_of(kt * tk, tk)
        return _dot(kaug_ref[g, pl.ds(off, tk), :], qaug[g])

    def slc_body(kt, carry):
        off = pl.multiple_of(kt * tk, tk)
        return tuple(attend(scores(g, kt), vst_ref[g, :, pl.ds(off, tk)], *carry[g]) for g in groups)

    carry = lax.fori_loop(0, i, slc_body, tuple((m0, l0, acc0) for g in groups))
    causal = jnp.where((i * tk + krow) <= t1, 0.0, NEG)
    causal4 = jnp.concatenate([causal] * NSA_HPG, axis=1)
    off_d = pl.multiple_of(i * tk, tk)
    o_slc = []
    for g in groups:
        _, l_fin, acc_fin = attend(scores(g, i) + causal4, vst_ref[g, :, pl.ds(off_d, tk)], *carry[g])
        o_slc.append(acc_fin / l_fin)

    eye_q = (lax.broadcasted_iota(jnp.int32, (tq, tq), 0)
             == lax.broadcasted_iota(jnp.int32, (tq, tq), 1)).astype(BF16)
    o_heads = []
    gt = gate_ref[...].T
    for g in groups:
        for h in range(NSA_HPG):
            cs = slice(h * tq, (h + 1) * tq)
            r = 3 * (NSA_HPG * g + h)
            o = (gt[r:r + 1, :] * o_cmp[g][:, cs] + gt[r + 1:r + 2, :] * o_slc[g][:, cs]
                 + gt[r + 2:r + 3, :] * o_win[g][:, cs])
            o_heads.append(o.astype(BF16))
    o_ref[...] = _transpose_bf16(eye_q, jnp.concatenate(o_heads, axis=0))


def _nsa(q_hm, cmp, kv4, gates, ovt, extn, batch, seq):
    t = q_hm.shape[1]
    tq = NSA_TQ
    assert tq % min(tq, WIN_TQ) == 0 and WINDOW % min(tq, WIN_TQ) == 0 and WINDOW + tq <= seq
    nq = seq // tq
    ncmp = cmp.shape[2]
    qblk = pl.BlockSpec((NSA_HEADS, tq, NSA_HEAD_DIM), lambda b, i: (0, b * nq + i, 0))
    return pl.pallas_call(
        functools.partial(_nsa_kernel, tq=tq),
        out_shape=jax.ShapeDtypeStruct((t, NSA_WIDTH), BF16),
        grid=(batch, nq),
        in_specs=[qblk,
                  pl.BlockSpec((None, 4, ncmp, NSA_HEAD_DIM), lambda b, i: (b, 0, 0, 0)),
                  pl.BlockSpec((8, seq, NSA_HEAD_DIM), lambda b, i: (0, b, 0)),
                  pl.BlockSpec((tq, LANES), lambda b, i: (b * nq + i, 0)),
                  _const_spec(ovt.shape), _const_spec(extn.shape)],
        out_specs=pl.BlockSpec((tq, NSA_WIDTH), lambda b, i: (b * nq + i, 0)),
        scratch_shapes=[pltpu.VMEM((NSA_KV_HEADS, seq, 2 * LANES), BF16),
                        pltpu.VMEM((NSA_KV_HEADS, NSA_HEAD_DIM, seq), BF16),
                        pltpu.VMEM((NSA_KV_HEADS, NSA_HEAD_DIM, seq), BF16)],
        compiler_params=_params("arbitrary", "arbitrary"),
        name="nsa",
    )(q_hm, cmp, kv4, gates, ovt, extn)


def _merge_kernel(x_ref, pm_ref, on_ref, gm_ref, gpre_ref, wmg_ref, bmg_ref, wbp_ref, wbn_ref, wbg_ref,
                  wout_ref, gpost_ref, o_ref):
    x = x_ref[...]
    d = x.shape[1]
    h = _rms(x, gpre_ref[...]).astype(BF16)

    def gate(j):
        return jax.nn.sigmoid(_dot(h, wmg_ref[:, j * d:(j + 1) * d]) + bmg_ref[:, j * d:(j + 1) * d])

    comb = gate(0) * _dot(pm_ref[...], wbp_ref[...])
    comb = comb + gate(1) * _dot(on_ref[...], wbn_ref[...])
    comb = comb + gate(2) * _dot(gm_ref[...], wbg_ref[...])
    y = _dot(comb.astype(BF16), wout_ref[...])
    o_ref[...] = x + _rms(y, gpost_ref[...])


def _merge(x, l, pm, o_nsa, gm, gpre, wmg, bmg, wbp, wbn, wbg, wout, gpost):
    t, d = x.shape
    tm = MERGE_TM
    row = lambda width: pl.BlockSpec((tm, width), lambda i: (i, 0))
    cs = functools.partial(_const_spec, layer=l)
    return pl.pallas_call(
        _merge_kernel,
        out_shape=jax.ShapeDtypeStruct((t, d), F32),
        grid=(t // tm,),
        in_specs=[row(d), row(POOL_WIDTH), row(NSA_WIDTH), row(GMLP_WIDTH), cs((1, d)), cs((d, 3 * d)),
                  cs((1, 3 * d)), cs((POOL_WIDTH, d)), cs((NSA_WIDTH, d)), cs((GMLP_WIDTH, d)), cs((d, d)),
                  cs((1, d))],
        out_specs=row(d),
        compiler_params=_params("parallel"),
        name="merge",
    )(x, pm, o_nsa, gm, gpre, wmg, bmg, wbp, wbn, wbg, wout, gpost)


def _memkv_kernel(m_ref, g_ref, wk_ref, wv_ref, k_ref, v_ref):
    mn = _rms(m_ref[...], g_ref[...]).astype(BF16)
    k_ref[...] = _dot(mn, wk_ref[...]).astype(BF16)
    v_ref[...] = _dot(mn, wv_ref[...]).astype(BF16)


def _memkv(mem, l, g, wk, wv):
    batch, mlen, d = mem.shape
    blk = pl.BlockSpec((None, mlen, d), lambda b: (b, 0, 0))
    out = jax.ShapeDtypeStruct((batch, mlen, d), BF16)
    cs = functools.partial(_const_spec, layer=l)
    return pl.pallas_call(
        _memkv_kernel,
        out_shape=(out, out),
        grid=(batch,),
        in_specs=[blk, cs((1, d)), cs((d, d)), cs((d, d))],
        out_specs=(blk, blk),
        compiler_params=_params("parallel"),
        name="memkv",
    )(mem, g, wk, wv)


def _memattn_kernel(x_ref, gpre_ref, wq_ref, k_ref, v_ref, wo_ref, gpost_ref, o_ref):
    x = x_ref[...]
    d = x.shape[1]
    hd = d // MEM_HEADS
    h = _rms(x, gpre_ref[...]).astype(BF16)
    q = (_dot(h, wq_ref[...]) * (hd ** -0.5)).astype(BF16)
    outs = []
    for n in range(MEM_HEADS):
        cs = slice(n * hd, (n + 1) * hd)
        s = _dot_nt(q[:, cs], k_ref[:, cs])
        e = jnp.exp(s - jnp.max(s, axis=-1, keepdims=True))
        p = e / jnp.sum(e, axis=-1, keepdims=True)
        outs.append(_dot(p.astype(BF16), v_ref[:, cs]).astype(BF16))
    y = _dot(jnp.concatenate(outs, axis=1), wo_ref[...])
    o_ref[...] = x + _rms(y, gpost_ref[...])


def _memattn(x, l, gpre, wq, k, v, wo, gpost, batch, seq):
    t, d = x.shape
    mlen = k.shape[1]
    tm = MEM_TM
    nt = seq // tm
    row = pl.BlockSpec((tm, d), lambda b, i: (b * nt + i, 0))
    kvb = pl.BlockSpec((None, mlen, d), lambda b, i: (b, 0, 0))
    cs = functools.partial(_const_spec, layer=l)
    return pl.pallas_call(
        _memattn_kernel,
        out_shape=jax.ShapeDtypeStruct((t, d), F32),
        grid=(batch, nt),
        in_specs=[row, cs((1, d)), cs((d, d)), kvb, kvb, cs((d, d)), cs((1, d))],
        out_specs=row,
        compiler_params=_params("parallel", "arbitrary"),
        name="memattn",
    )(x, gpre, wq, k, v, wo, gpost)


def _overlap_matrix_t(n_cmp_pad, n_sel):
    cs = np.arange(n_cmp_pad)[None, :] * CMP_STRIDE
    ss = np.arange(LANES)[:, None] * SEL_BLOCK
    ov = np.minimum(cs + CMP_BLOCK, ss + SEL_BLOCK) - np.maximum(cs, ss)
    ov = np.maximum(ov, 0) / CMP_BLOCK
    ov[n_sel:, :] = 0.0
    return jnp.asarray(ov, dtype=BF16)


def _block_mask_matrix(seq):
    ex = (np.arange(seq)[:, None] // SEL_BLOCK) == np.arange(LANES)[None, :]
    return jnp.asarray(np.where(ex, -MASK_BIG, 0.0), dtype=BF16)


def kernel(x, mem, ff1_pre_g, ff1_w1, ff1_w3, ff1_w2, ff1_post_g, mix_pre_g, w_in, b_in, pool_w, pool_scale, cmp_pos_k, cmp_w1_k, cmp_w2_k, cmp_pos_v, cmp_w1_v, cmp_w2_v, gmlp_ln_g, gmlp_ln_b, gmlp_ws, gmlp_bs, w_br_pool, w_br_nsa, w_br_gmlp, w_mix_out, mix_post_g, mem_pre_g, mem_kv_g, mem_wq, mem_wk, mem_wv, mem_wo, mem_post_g, ff2_pre_g, ff2_w1, ff2_w3, ff2_w2, ff2_post_g):
    batch, seq, d = x.shape
    depth = w_in.shape[0]
    t = batch * seq
    n_cmp_rows = seq // CMP_STRIDE
    bf = lambda w: w.astype(BF16)
    vec = lambda g: g.reshape(depth, 1, -1)

    c_g = 1792
    c_gm = c_g + 3 * NSA_HEADS
    c_mg = c_gm + 2 * GMLP_WIDTH
    w_in_b = bf(w_in)
    n_gate = c_gm - c_g
    w_proj = w_in_b[:, :, :c_g]
    b_proj = b_in[:, :c_g].reshape(depth, 1, -1)
    w_gate = jnp.concatenate([w_in_b[:, :, c_g:c_gm], jnp.zeros((depth, d, LANES - n_gate), BF16)], axis=2)
    b_gate = jnp.concatenate([b_in[:, c_g:c_gm], jnp.zeros((depth, LANES - n_gate), b_in.dtype)],
                             axis=1).reshape(depth, 1, -1)
    w_gm = w_in_b[:, :, c_gm:c_mg]
    b_gm = b_in[:, c_gm:c_mg].reshape(depth, 1, -1)
    w_mg = w_in_b[:, :, c_mg:]
    b_mg = b_in[:, c_mg:].reshape(depth, 1, -1)
    bsx = jnp.broadcast_to(jnp.swapaxes(gmlp_bs, 1, 2)[:, :, :, None],
                           (depth, GMLP_CHUNK, GMLP_GROUPS, GMLP_WIDTH // GMLP_GROUPS)).reshape(
                               depth, GMLP_CHUNK, GMLP_WIDTH)
    cmp_pos = jnp.stack([cmp_pos_k, cmp_pos_v], axis=1)
    cmp_pos = jnp.concatenate([cmp_pos, cmp_pos], axis=-1)
    cmp_w1 = bf(jnp.stack([cmp_w1_k, cmp_w1_v], axis=1))
    w1_zero = jnp.zeros_like(cmp_w1)
    cmp_w1 = jnp.concatenate([jnp.concatenate([cmp_w1, w1_zero], axis=-1),
                              jnp.concatenate([w1_zero, cmp_w1], axis=-1)], axis=-2)
    cmp_w2 = bf(jnp.stack([cmp_w2_k, cmp_w2_v], axis=1))
    w_br_nsa_b = bf(w_br_nsa)
    ff1 = (vec(ff1_pre_g), bf(ff1_w1), bf(ff1_w3), bf(ff1_w2), vec(ff1_post_g))
    ff2 = (vec(ff2_pre_g), bf(ff2_w1), bf(ff2_w3), bf(ff2_w2), vec(ff2_post_g))
    pool_w_b, pool_scale_v = bf(pool_w), vec(pool_scale)
    w_br_pool_b, w_br_gmlp_b, w_mix_out_b = bf(w_br_pool), bf(w_br_gmlp), bf(w_mix_out)
    mem_wq_b, mem_wk_b, mem_wv_b, mem_wo_b = bf(mem_wq), bf(mem_wk), bf(mem_wv), bf(mem_wo)
    mix_pre, mix_post = vec(mix_pre_g), vec(mix_post_g)
    mem_pre, mem_kv, mem_post = vec(mem_pre_g), vec(mem_kv_g), vec(mem_post_g)
    ln_g, ln_b = vec(gmlp_ln_g), vec(gmlp_ln_b)
    ov = _overlap_matrix_t(n_cmp_rows, seq // SEL_BLOCK)
    ex = _block_mask_matrix(seq)

    xs = x.reshape(t, d)
    for l in range(depth):
        xs = _ffn(xs, l, *ff1)
        a, q_hm, kvc, kv4, gates, gm = _inproj(xs, l, mix_pre, w_proj, b_proj, w_gate, b_gate, w_gm, b_gm,
                                               gmlp_ws, ln_g, ln_b, bsx)
        pm = _pool(a, l, pool_w_b, pool_scale_v, batch, seq)
        cmp = _cmpkv(kvc, l, cmp_pos, cmp_w1, cmp_w2, batch, seq)
        o_nsa = _nsa(q_hm, cmp, kv4, gates, ov, ex, batch, seq)
        xs = _merge(xs, l, pm, o_nsa, gm, mix_pre, w_mg, b_mg, w_br_pool_b, w_br_nsa_b, w_br_gmlp_b,
                    w_mix_out_b, mix_post)
        mk, mv = _memkv(mem, l, mem_kv, mem_wk_b, mem_wv_b)
        xs = _memattn(xs, l, mem_pre, mem_wq_b, mk, mv, mem_wo_b, mem_post, batch, seq)
        xs = _ffn(xs, l, *ff2)
    return xs.reshape(batch, seq, d)
```

```python
import functools

import numpy as np
import jax
import jax.numpy as jnp
from jax import lax
from jax.experimental import pallas as pl
from jax.experimental.pallas import tpu as pltpu

F32 = jnp.float32
BF16 = jnp.bfloat16

EPS = 1e-6
NEG = -1e30
MASK_BIG = 2.0 ** 100
LOG2E = float(np.log2(np.e))

MEM_HEADS = 4
POOL_WINDOWS = (2, 4, 8, 16)
POOL_GROUP = 128
NSA_HEADS = 8
NSA_KV_HEADS = 2
NSA_HPG = NSA_HEADS // NSA_KV_HEADS
NSA_HEAD_DIM = 64
CMP_BLOCK = 32
CMP_STRIDE = 16
SEL_BLOCK = 64
SEL_TOPK = 8
WINDOW = 256
GMLP_WIDTH = 512
GMLP_GROUPS = 4
GMLP_CHUNK = 128
POOL_WIDTH = 512
NSA_WIDTH = NSA_HEADS * NSA_HEAD_DIM

LANES = 128
VMEM_LIMIT_BYTES = 56 * 1024 * 1024

FFN_TM = 512
FFN_CHUNK = 512
PROJ_TM = 512
MERGE_TM = 512
MEM_TM = 1024
NSA_TQ = 512
WIN_TQ = 128


def _dot(a, b):
    return jnp.dot(a, b, preferred_element_type=F32)


def _dot_nt(a, b):
    return lax.dot_general(a, b, (((1,), (1,)), ((), ())), preferred_element_type=F32)


def _rms(x, g):
    return x * lax.rsqrt(jnp.mean(x * x, axis=-1, keepdims=True) + EPS) * g


def _gelu(x):
    c = np.float32(np.sqrt(2.0 / np.pi))
    return x * (0.5 * (1.0 + jnp.tanh(c * (x + 0.044715 * (x * x * x)))))


def _const_spec(shape, layer=None):
    n = len(shape)
    if layer is None:
        return pl.BlockSpec(shape, lambda *_: (0,) * n, pipeline_mode=pl.Buffered(1))
    return pl.BlockSpec((None,) + tuple(shape), lambda *_: (layer,) + (0,) * n, pipeline_mode=pl.Buffered(1))


def _params(*sem):
    return pltpu.CompilerParams(dimension_semantics=sem, vmem_limit_bytes=VMEM_LIMIT_BYTES)


def _ffn_kernel(x_ref, gpre_ref, w1_ref, w3_ref, w2_ref, gpost_ref, o_ref, h_ref, acc_ref, *, chunks):
    x = x_ref[...]
    h_ref[...] = _rms(x, gpre_ref[...]).astype(BF16)
    for n, (off, width) in enumerate(chunks):
        h = h_ref[...]
        a = _dot(h, w1_ref[:, off:off + width])
        b = _dot(h, w3_ref[:, off:off + width])
        u = (a * jax.nn.sigmoid(a) * b).astype(BF16)
        y = _dot(u, w2_ref[off:off + width, :])
        if n == 0:
            acc_ref[...] = y
        else:
            acc_ref[...] += y
    o_ref[...] = x + 0.5 * _rms(acc_ref[...], gpost_ref[...])


def _ffn(x, l, gpre, w1, w3, w2, gpost):
    t, d = x.shape
    f = w1.shape[-1]
    tm = FFN_TM
    cs = functools.partial(_const_spec, layer=l)
    chunks = tuple((off, min(FFN_CHUNK, f - off)) for off in range(0, f, FFN_CHUNK))
    row = pl.BlockSpec((tm, d), lambda i: (i, 0))
    return pl.pallas_call(
        functools.partial(_ffn_kernel, chunks=chunks),
        out_shape=jax.ShapeDtypeStruct((t, d), F32),
        grid=(t // tm,),
        in_specs=[row, cs((1, d)), cs((d, f)), cs((d, f)), cs((f, d)), cs((1, d))],
        out_specs=row,
        scratch_shapes=[pltpu.VMEM((tm, d), BF16), pltpu.VMEM((tm, d), F32)],
        compiler_params=_params("parallel"),
        name="ffn",
    )(x, gpre, w1, w3, w2, gpost)


_A0, _Q0, _KV0, _PROJ_COLS = 0, 512, 1024, 1792


def _inproj_kernel(x_ref, g_ref, w_ref, b_ref, wg_ref, bg_ref, wgm_ref, bgm_ref, ws_ref, lng_ref, lnb_ref, bsx_ref,
                   a_ref, q_ref, kvc_ref, kv4_ref, gate_ref, gm_ref, *, tm):
    h = _rms(x_ref[...], g_ref[...]).astype(BF16)
    z = _dot(h, w_ref[...]) + b_ref[...]
    a_ref[...] = z[:, _A0:_A0 + POOL_WIDTH]
    zq = z[:, _Q0:_Q0 + NSA_WIDTH] * (NSA_HEAD_DIM ** -0.5 * LOG2E)
    for hh in range(NSA_HEADS):
        q_ref[hh] = zq[:, hh * 64:(hh + 1) * 64].astype(BF16)
    kvc_ref[0] = z[:, _KV0:_KV0 + LANES]
    kvc_ref[1] = z[:, _KV0 + LANES:_KV0 + 2 * LANES]
    for r in range(8):
        c0 = _KV0 + 256 + r * 64
        kv4_ref[r] = z[:, c0:c0 + 64].astype(BF16)
    gate_ref[...] = jax.nn.sigmoid(_dot(h, wg_ref[...]) + bg_ref[...])
    gm = _gelu(_dot(h, wgm_ref[...]) + bgm_ref[...])
    u = gm[:, :GMLP_WIDTH]
    v = gm[:, GMLP_WIDTH:]
    mu = jnp.mean(v, axis=-1, keepdims=True)
    vc = v - mu
    var = jnp.mean(vc * vc, axis=-1, keepdims=True)
    vn = (vc * lax.rsqrt(var + EPS) * lng_ref[...] + lnb_ref[...]).astype(BF16)
    ri = lax.broadcasted_iota(jnp.int32, (GMLP_CHUNK, GMLP_CHUNK), 0)
    ci = lax.broadcasted_iota(jnp.int32, (GMLP_CHUNK, GMLP_CHUNK), 1)
    for g in range(GMLP_GROUPS):
        wg = jnp.where(ri >= ci, ws_ref[g], 0.0).astype(BF16)
        cs = slice(g * 128, (g + 1) * 128)
        for c in range(tm // GMLP_CHUNK):
            rs = slice(c * GMLP_CHUNK, (c + 1) * GMLP_CHUNK)
            s = _dot(wg, vn[rs, cs]) + bsx_ref[:, cs]
            gm_ref[rs, cs] = (u[rs, cs] * s).astype(BF16)


def _inproj(x, l, g, w, b, wg, bg, wgm, bgm, ws, lng, lnb, bsx):
    t, d = x.shape
    tm = PROJ_TM
    cs = functools.partial(_const_spec, layer=l)
    row = lambda width: pl.BlockSpec((tm, width), lambda i: (i, 0))
    heads = pl.BlockSpec((8, tm, 64), lambda i: (0, i, 0))
    return pl.pallas_call(
        functools.partial(_inproj_kernel, tm=tm),
        out_shape=(
            jax.ShapeDtypeStruct((t, POOL_WIDTH), F32),
            jax.ShapeDtypeStruct((8, t, 64), BF16),
            jax.ShapeDtypeStruct((2, t, LANES), F32),
            jax.ShapeDtypeStruct((8, t, 64), BF16),
            jax.ShapeDtypeStruct((t, LANES), F32),
            jax.ShapeDtypeStruct((t, GMLP_WIDTH), BF16),
        ),
        grid=(t // tm,),
        in_specs=[row(d), cs((1, d)), cs((d, _PROJ_COLS)), cs((1, _PROJ_COLS)), cs((d, LANES)), cs((1, LANES)),
                  cs((d, 2 * GMLP_WIDTH)), cs((1, 2 * GMLP_WIDTH)),
                  cs((GMLP_GROUPS, GMLP_CHUNK, GMLP_CHUNK)), cs((1, GMLP_WIDTH)), cs((1, GMLP_WIDTH)),
                  cs((GMLP_CHUNK, GMLP_WIDTH))],
        out_specs=(row(POOL_WIDTH), heads, pl.BlockSpec((2, tm, LANES), lambda i: (0, i, 0)), heads, row(LANES),
                   row(GMLP_WIDTH)),
        compiler_params=_params("parallel"),
        name="inproj",
    )(x, g, w, b, wg, bg, wgm, bgm, ws, lng, lnb, bsx)


_POOL_PAD = 16


def _pool_kernel(a_ref, pw_ref, ps_ref, o_ref, pad_ref, *, seq):
    pos = lax.broadcasted_iota(jnp.int32, (seq, 1), 0).astype(F32) + 1.0
    pad_ref[0:_POOL_PAD, :] = jnp.zeros((_POOL_PAD, POOL_GROUP), F32)
    for gi, w in enumerate(POOL_WINDOWS):
        cs = slice(gi * POOL_GROUP, (gi + 1) * POOL_GROUP)
        x = a_ref[:, cs]
        s = x
        k = 1
        while k < w:
            pad_ref[_POOL_PAD:_POOL_PAD + seq, :] = s
            s = s + pad_ref[_POOL_PAD - k:_POOL_PAD - k + seq, :]
            k *= 2
        pooled = (s / jnp.minimum(pos, float(w)) - x).astype(BF16)
        o_ref[:, cs] = (_dot(pooled, pw_ref[gi]) * ps_ref[:, cs]).astype(BF16)


def _pool(a, l, pw, ps, batch, seq):
    t = a.shape[0]
    blk = pl.BlockSpec((seq, POOL_WIDTH), lambda b: (b, 0))
    cs = functools.partial(_const_spec, layer=l)
    return pl.pallas_call(
        functools.partial(_pool_kernel, seq=seq),
        out_shape=jax.ShapeDtypeStruct((t, POOL_WIDTH), BF16),
        grid=(batch,),
        in_specs=[blk, cs((len(POOL_WINDOWS), POOL_GROUP, POOL_GROUP)), cs((1, POOL_WIDTH))],
        out_specs=blk,
        scratch_shapes=[pltpu.VMEM((_POOL_PAD + seq, POOL_GROUP), F32)],
        compiler_params=_params("parallel"),
        name="pool",
    )(a, pw, ps)


def _cmpkv_kernel(x_ref, pos_ref, w1_ref, w2_ref, o_ref):
    nrow = x_ref.shape[0] // CMP_STRIDE
    p = q = None
    for l in range(CMP_STRIDE):
        xl = x_ref[pl.ds(l, nrow, stride=CMP_STRIDE), :]
        pl_ = _dot((xl + pos_ref[l:l + 1, :]).astype(BF16), w1_ref[l])
        ql = _dot((xl + pos_ref[CMP_STRIDE + l:CMP_STRIDE + l + 1, :]).astype(BF16), w1_ref[CMP_STRIDE + l])
        p = pl_ if p is None else p + pl_
        q = ql if q is None else q + ql
    hpre = p + pltpu.roll(q, nrow - 1, axis=0)
    h = _gelu(hpre).astype(BF16)
    hid = w2_ref.shape[0]
    for g in range(NSA_KV_HEADS):
        o_ref[g] = _dot(h[:, g * hid:(g + 1) * hid], w2_ref[...]).astype(BF16)


def _cmpkv(kvc, l, pos, w1, w2, batch, seq):
    nrow = seq // CMP_STRIDE
    hid = w2.shape[-2]
    return pl.pallas_call(
        _cmpkv_kernel,
        out_shape=jax.ShapeDtypeStruct((batch, 2 * NSA_KV_HEADS, nrow, NSA_HEAD_DIM), BF16),
        grid=(2, batch),
        in_specs=[pl.BlockSpec((None, seq, LANES), lambda r, b: (r, b, 0)),
                  pl.BlockSpec((None, None, CMP_BLOCK, LANES), lambda r, b: (l, r, 0, 0)),
                  pl.BlockSpec((None, None, CMP_BLOCK, LANES, NSA_KV_HEADS * hid), lambda r, b: (l, r, 0, 0, 0)),
                  pl.BlockSpec((None, None, hid, NSA_HEAD_DIM), lambda r, b: (l, r, 0, 0))],
        out_specs=pl.BlockSpec((None, NSA_KV_HEADS, nrow, NSA_HEAD_DIM), lambda r, b: (b, r, 0, 0)),
        compiler_params=_params("arbitrary", "arbitrary"),
        name="cmpkv",
    )(kvc, pos, w1, w2)


def _transpose_bf16(eye, x):
    return _dot_nt(eye, x).astype(BF16)


def _nsa_kernel(q_ref, cmp_ref, kv_ref, gate_ref, ovt_ref, extn_ref, o_ref, kaug_ref, vst_ref, vwt_ref, *, tq):
    i = pl.program_id(1)
    tk = tq
    hd = NSA_HEAD_DIM
    m4 = NSA_HPG * tq
    seq = kv_ref.shape[1]
    n_sel = seq // SEL_BLOCK
    groups = range(NSA_KV_HEADS)
    eye = (lax.broadcasted_iota(jnp.int32, (hd, hd), 0)
           == lax.broadcasted_iota(jnp.int32, (hd, hd), 1)).astype(BF16)

    @pl.when(i == 0)
    def _():
        place = (lax.broadcasted_iota(jnp.int32, (hd, LANES), 0)
                 == lax.broadcasted_iota(jnp.int32, (hd, LANES), 1)).astype(BF16)
        for g in groups:
            kaug_ref[g, :, 0:LANES] = extn_ref[...]
            for kt in range(seq // tk):
                rs = slice(kt * tk, (kt + 1) * tk)
                kaug_ref[g, rs, LANES:2 * LANES] = _dot(kv_ref[g, rs, :], place).astype(BF16)
                vst_ref[g, :, rs] = _transpose_bf16(eye, kv_ref[2 + g, rs, :])
                vwt_ref[g, :, rs] = _transpose_bf16(eye, kv_ref[6 + g, rs, :])

    col = lax.broadcasted_iota(jnp.int32, (1, m4), 1)
    t4 = i * tq + jnp.bitwise_and(col, tq - 1)
    t1 = i * tq + lax.broadcasted_iota(jnp.int32, (1, tq), 1)
    krow = lax.broadcasted_iota(jnp.int32, (tk, 1), 0)
    m0 = jnp.full((1, m4), NEG, F32)
    l0 = jnp.zeros((1, m4), F32)
    acc0 = jnp.zeros((hd, m4), F32)

    def attend(s, vt_tile, m_run, l_run, acc):
        m_new = jnp.maximum(m_run, jnp.max(s, axis=0, keepdims=True))
        alpha = jnp.exp2(m_run - m_new)
        e = jnp.exp2(s - m_new)
        l_new = alpha * l_run + jnp.sum(e, axis=0, keepdims=True)
        return m_new, l_new, alpha * acc + _dot(vt_tile, e.astype(BF16))

    qt = [jnp.concatenate([_transpose_bf16(eye, q_ref[NSA_HPG * g + h]) for h in range(NSA_HPG)], axis=1)
          for g in groups]

    tw = min(tq, WIN_TQ)
    n_w = WINDOW + tw
    win_bias, win_off = [], []
    for j in range(tq // tw):
        off = pl.multiple_of(jnp.maximum(i * tq + j * tw - WINDOW, 0), tw)
        kp = off + lax.broadcasted_iota(jnp.int32, (n_w, 1), 0)
        tj = t1[:, j * tw:(j + 1) * tw]
        bias = jnp.where((kp > tj - WINDOW) & (kp <= tj), 0.0, NEG)
        win_bias.append(jnp.concatenate([bias] * NSA_HPG, axis=1))
        win_off.append(off)

    def window(g):
        parts = []
        for j in range(tq // tw):
            cols = [qt[g][:, h * tq + j * tw:h * tq + (j + 1) * tw] for h in range(NSA_HPG)]
            s = _dot(kv_ref[4 + g, pl.ds(win_off[j], n_w), :], jnp.concatenate(cols, axis=1)) + win_bias[j]
            e = jnp.exp2(s - jnp.max(s, axis=0, keepdims=True))
            acc = _dot(vwt_ref[g, :, pl.ds(win_off[j], n_w)], e.astype(BF16))
            parts.append(acc / jnp.sum(e, axis=0, keepdims=True))
        return jnp.concatenate([p[:, h * tw:(h + 1) * tw] for h in range(NSA_HPG) for p in parts], axis=1)

    def compressed(g):
        ncmp = cmp_ref.shape[1]
        crow = lax.broadcasted_iota(jnp.int32, (ncmp, 1), 0)
        valid = (crow * CMP_STRIDE + (CMP_BLOCK - 1)) <= t4
        s = jnp.where(valid, _dot(cmp_ref[g], qt[g]), NEG)
        e = jnp.exp2(s - jnp.max(s, axis=0, keepdims=True))
        p = jnp.where(valid, e / jnp.sum(e, axis=0, keepdims=True), 0.0)
        o_cmp = _dot(_transpose_bf16(eye, cmp_ref[2 + g]), p.astype(BF16))
        psum = p[:, 0:tq] + p[:, tq:2 * tq] + p[:, 2 * tq:3 * tq] + p[:, 3 * tq:4 * tq]
        p_hi = psum.astype(BF16)
        p_lo = (psum - p_hi.astype(F32)).astype(BF16)
        imp = (_dot(ovt_ref[...], p_hi) + _dot(ovt_ref[...], p_lo))[0:n_sel, :]
        jrow = lax.broadcasted_iota(jnp.int32, (n_sel, 1), 0)
        cur = jnp.right_shift(t1, 6)
        forced = (jrow == 0) | (jrow == cur) | (jrow == cur - 1)
        vt = jnp.where(forced, 1e9, jnp.where(jrow * SEL_BLOCK <= t1, imp, -1.0))
        n_acc = 4
        cnt = [jnp.zeros((n_sel, tq), F32) for _ in range(n_acc)]
        for jp in range(n_sel):
            r = vt[jp:jp + 1, :]
            tie = jnp.where(jrow > jp, 1.0, 0.0)
            cnt[jp % n_acc] = cnt[jp % n_acc] + jnp.where(r > vt, 1.0, jnp.where(r == vt, tie, 0.0))
        rank = (cnt[0] + cnt[1]) + (cnt[2] + cnt[3])
        notsel = jnp.where(rank < float(SEL_TOPK), 0.0, 1.0)
        notsel = jnp.concatenate([notsel, jnp.zeros((LANES - n_sel, tq), F32)], axis=0).astype(BF16)
        qaug = jnp.concatenate([jnp.concatenate([notsel] * NSA_HPG, axis=1), qt[g], jnp.zeros((hd, m4), BF16)],
                               axis=0)
        return o_cmp, qaug

    o_win = [window(g) for g in groups]
    o_cmp, qaug = zip(*[compressed(g) for g in groups])

    def scores(g, kt):
        off = pl.multiple_of(kt * tk, tk)
        return _dot(kaug_ref[g, pl.ds(off, tk), :], qaug[g])

    def slc_body(kt, carry):
        off = pl.multiple_of(kt * tk, tk)
        return tuple(attend(scores(g, kt), vst_ref[g, :, pl.ds(off, tk)], *carry[g]) for g in groups)

    carry = lax.fori_loop(0, i, slc_body, tuple((m0, l0, acc0) for g in groups))
    causal = jnp.where((i * tk + krow) <= t1, 0.0, NEG)
    causal4 = jnp.concatenate([causal] * NSA_HPG, axis=1)
    off_d = pl.multiple_of(i * tk, tk)
    o_slc = []
    for g in groups:
        _, l_fin, acc_fin = attend(scores(g, i) + causal4, vst_ref[g, :, pl.ds(off_d, tk)], *carry[g])
        o_slc.append(acc_fin / l_fin)

    eye_q = (lax.broadcasted_iota(jnp.int32, (tq, tq), 0)
             == lax.broadcasted_iota(jnp.int32, (tq, tq), 1)).astype(BF16)
    o_heads = []
    gt = gate_ref[...].T
    for g in groups:
        for h in range(NSA_HPG):
            cs = slice(h * tq, (h + 1) * tq)
            r = 3 * (NSA_HPG * g + h)
            o = (gt[r:r + 1, :] * o_cmp[g][:, cs] + gt[r + 1:r + 2, :] * o_slc[g][:, cs]
                 + gt[r + 2:r + 3, :] * o_win[g][:, cs])
            o_heads.append(o.astype(BF16))
    o_ref[...] = _transpose_bf16(eye_q, jnp.concatenate(o_heads, axis=0))


def _nsa(q_hm, cmp, kv4, gates, ovt, extn, batch, seq):
    t = q_hm.shape[1]
    tq = NSA_TQ
    assert tq % min(tq, WIN_TQ) == 0 and WINDOW % min(tq, WIN_TQ) == 0 and WINDOW + tq <= seq
    nq = seq // tq
    ncmp = cmp.shape[2]
    qblk = pl.BlockSpec((NSA_HEADS, tq, NSA_HEAD_DIM), lambda b, i: (0, b * nq + i, 0))
    return pl.pallas_call(
        functools.partial(_nsa_kernel, tq=tq),
        out_shape=jax.ShapeDtypeStruct((t, NSA_WIDTH), BF16),
        grid=(batch, nq),
        in_specs=[qblk,
                  pl.BlockSpec((None, 4, ncmp, NSA_HEAD_DIM), lambda b, i: (b, 0, 0, 0)),
                  pl.BlockSpec((8, seq, NSA_HEAD_DIM), lambda b, i: (0, b, 0)),
                  pl.BlockSpec((tq, LANES), lambda b, i: (b * nq + i, 0)),
                  _const_spec(ovt.shape), _const_spec(extn.shape)],
        out_specs=pl.BlockSpec((tq, NSA_WIDTH), lambda b, i: (b * nq + i, 0)),
        scratch_shapes=[pltpu.VMEM((NSA_KV_HEADS, seq, 2 * LANES), BF16),
                        pltpu.VMEM((NSA_KV_HEADS, NSA_HEAD_DIM, seq), BF16),
                        pltpu.VMEM((NSA_KV_HEADS, NSA_HEAD_DIM, seq), BF16)],
        compiler_params=_params("arbitrary", "arbitrary"),
        name="nsa",
    )(q_hm, cmp, kv4, gates, ovt, extn)


def _merge_kernel(x_ref, pm_ref, on_ref, gm_ref, gpre_ref, wmg_ref, bmg_ref, wbp_ref, wbn_ref, wbg_ref,
                  wout_ref, gpost_ref, o_ref):
    x = x_ref[...]
    d = x.shape[1]
    h = _rms(x, gpre_ref[...]).astype(BF16)

    def gate(j):
        return jax.nn.sigmoid(_dot(h, wmg_ref[:, j * d:(j + 1) * d]) + bmg_ref[:, j * d:(j + 1) * d])

    comb = gate(0) * _dot(pm_ref[...], wbp_ref[...])
    comb = comb + gate(1) * _dot(on_ref[...], wbn_ref[...])
    comb = comb + gate(2) * _dot(gm_ref[...], wbg_ref[...])
    y = _dot(comb.astype(BF16), wout_ref[...])
    o_ref[...] = x + _rms(y, gpost_ref[...])


def _merge(x, l, pm, o_nsa, gm, gpre, wmg, bmg, wbp, wbn, wbg, wout, gpost):
    t, d = x.shape
    tm = MERGE_TM
    row = lambda width: pl.BlockSpec((tm, width), lambda i: (i, 0))
    cs = functools.partial(_const_spec, layer=l)
    return pl.pallas_call(
        _merge_kernel,
        out_shape=jax.ShapeDtypeStruct((t, d), F32),
        grid=(t // tm,),
        in_specs=[row(d), row(POOL_WIDTH), row(NSA_WIDTH), row(GMLP_WIDTH), cs((1, d)), cs((d, 3 * d)),
                  cs((1, 3 * d)), cs((POOL_WIDTH, d)), cs((NSA_WIDTH, d)), cs((GMLP_WIDTH, d)), cs((d, d)),
                  cs((1, d))],
        out_specs=row(d),
        compiler_params=_params("parallel"),
        name="merge",
    )(x, pm, o_nsa, gm, gpre, wmg, bmg, wbp, wbn, wbg, wout, gpost)


def _memattn_kernel(x_ref, gpre_ref, wq_ref, m_ref, gkv_ref, wk_ref, wv_ref, wo_ref, gpost_ref, o_ref,
                    k_ref, v_ref):
    @pl.when(pl.program_id(1) == 0)
    def _():
        mn = _rms(m_ref[...], gkv_ref[...]).astype(BF16)
        k_ref[...] = _dot(mn, wk_ref[...]).astype(BF16)
        v_ref[...] = _dot(mn, wv_ref[...]).astype(BF16)

    x = x_ref[...]
    d = x.shape[1]
    hd = d // MEM_HEADS
    h = _rms(x, gpre_ref[...]).astype(BF16)
    q = (_dot(h, wq_ref[...]) * (hd ** -0.5)).astype(BF16)
    outs = []
    for n in range(MEM_HEADS):
        cs = slice(n * hd, (n + 1) * hd)
        s = _dot_nt(q[:, cs], k_ref[:, cs])
        e = jnp.exp(s - jnp.max(s, axis=-1, keepdims=True))
        p = e / jnp.sum(e, axis=-1, keepdims=True)
        outs.append(_dot(p.astype(BF16), v_ref[:, cs]).astype(BF16))
    y = _dot(jnp.concatenate(outs, axis=1), wo_ref[...])
    o_ref[...] = x + _rms(y, gpost_ref[...])


def _memattn(x, l, gpre, wq, mem, gkv, wk, wv, wo, gpost, batch, seq):
    t, d = x.shape
    mlen = mem.shape[1]
    tm = MEM_TM
    nt = seq // tm
    row = pl.BlockSpec((tm, d), lambda b, i: (b * nt + i, 0))
    memb = pl.BlockSpec((None, mlen, d), lambda b, i: (b, 0, 0))
    cs = functools.partial(_const_spec, layer=l)
    return pl.pallas_call(
        _memattn_kernel,
        out_shape=jax.ShapeDtypeStruct((t, d), F32),
        grid=(batch, nt),
        in_specs=[row, cs((1, d)), cs((d, d)), memb, cs((1, d)), cs((d, d)), cs((d, d)), cs((d, d)), cs((1, d))],
        out_specs=row,
        scratch_shapes=[pltpu.VMEM((mlen, d), BF16), pltpu.VMEM((mlen, d), BF16)],
        compiler_params=_params("arbitrary", "arbitrary"),
        name="memattn",
    )(x, gpre, wq, mem, gkv, wk, wv, wo, gpost)


def _overlap_matrix_t(n_cmp_pad, n_sel):
    cs = np.arange(n_cmp_pad)[None, :] * CMP_STRIDE
    ss = np.arange(LANES)[:, None] * SEL_BLOCK
    ov = np.minimum(cs + CMP_BLOCK, ss + SEL_BLOCK) - np.maximum(cs, ss)
    ov = np.maximum(ov, 0) / CMP_BLOCK
    ov[n_sel:, :] = 0.0
    return jnp.asarray(ov, dtype=BF16)


def _block_mask_matrix(seq):
    ex = (np.arange(seq)[:, None] // SEL_BLOCK) == np.arange(LANES)[None, :]
    return jnp.asarray(np.where(ex, -MASK_BIG, 0.0), dtype=BF16)


def kernel(x, mem, ff1_pre_g, ff1_w1, ff1_w3, ff1_w2, ff1_post_g, mix_pre_g, w_in, b_in, pool_w, pool_scale, cmp_pos_k, cmp_w1_k, cmp_w2_k, cmp_pos_v, cmp_w1_v, cmp_w2_v, gmlp_ln_g, gmlp_ln_b, gmlp_ws, gmlp_bs, w_br_pool, w_br_nsa, w_br_gmlp, w_mix_out, mix_post_g, mem_pre_g, mem_kv_g, mem_wq, mem_wk, mem_wv, mem_wo, mem_post_g, ff2_pre_g, ff2_w1, ff2_w3, ff2_w2, ff2_post_g):
    batch, seq, d = x.shape
    depth = w_in.shape[0]
    t = batch * seq
    n_cmp_rows = seq // CMP_STRIDE
    bf = lambda w: w.astype(BF16)
    vec = lambda g: g.reshape(depth, 1, -1)

    c_g = 1792
    c_gm = c_g + 3 * NSA_HEADS
    c_mg = c_gm + 2 * GMLP_WIDTH
    w_in_b = bf(w_in)
    n_gate = c_gm - c_g
    w_proj = w_in_b[:, :, :c_g]
    b_proj = b_in[:, :c_g].reshape(depth, 1, -1)
    w_gate = jnp.concatenate([w_in_b[:, :, c_g:c_gm], jnp.zeros((depth, d, LANES - n_gate), BF16)], axis=2)
    b_gate = jnp.concatenate([b_in[:, c_g:c_gm], jnp.zeros((depth, LANES - n_gate), b_in.dtype)],
                             axis=1).reshape(depth, 1, -1)
    w_gm = w_in_b[:, :, c_gm:c_mg]
    b_gm = b_in[:, c_gm:c_mg].reshape(depth, 1, -1)
    w_mg = w_in_b[:, :, c_mg:]
    b_mg = b_in[:, c_mg:].reshape(depth, 1, -1)
    bsx = jnp.broadcast_to(jnp.swapaxes(gmlp_bs, 1, 2)[:, :, :, None],
                           (depth, GMLP_CHUNK, GMLP_GROUPS, GMLP_WIDTH // GMLP_GROUPS)).reshape(
                               depth, GMLP_CHUNK, GMLP_WIDTH)
    cmp_pos = jnp.stack([cmp_pos_k, cmp_pos_v], axis=1)
    cmp_pos = jnp.concatenate([cmp_pos, cmp_pos], axis=-1)
    cmp_w1 = bf(jnp.stack([cmp_w1_k, cmp_w1_v], axis=1))
    w1_zero = jnp.zeros_like(cmp_w1)
    cmp_w1 = jnp.concatenate([jnp.concatenate([cmp_w1, w1_zero], axis=-1),
                              jnp.concatenate([w1_zero, cmp_w1], axis=-1)], axis=-2)
    cmp_w2 = bf(jnp.stack([cmp_w2_k, cmp_w2_v], axis=1))
    w_br_nsa_b = bf(w_br_nsa)
    ff1 = (vec(ff1_pre_g), bf(ff1_w1), bf(ff1_w3), bf(ff1_w2), vec(ff1_post_g))
    ff2 = (vec(ff2_pre_g), bf(ff2_w1), bf(ff2_w3), bf(ff2_w2), vec(ff2_post_g))
    pool_w_b, pool_scale_v = bf(pool_w), vec(pool_scale)
    w_br_pool_b, w_br_gmlp_b, w_mix_out_b = bf(w_br_pool), bf(w_br_gmlp), bf(w_mix_out)
    mem_wq_b, mem_wk_b, mem_wv_b, mem_wo_b = bf(mem_wq), bf(mem_wk), bf(mem_wv), bf(mem_wo)
    mix_pre, mix_post = vec(mix_pre_g), vec(mix_post_g)
    mem_pre, mem_kv, mem_post = vec(mem_pre_g), vec(mem_kv_g), vec(mem_post_g)
    ln_g, ln_b = vec(gmlp_ln_g), vec(gmlp_ln_b)
    ov = _overlap_matrix_t(n_cmp_rows, seq // SEL_BLOCK)
    ex = _block_mask_matrix(seq)

    xs = x.reshape(t, d)
    for l in range(depth):
        xs = _ffn(xs, l, *ff1)
        a, q_hm, kvc, kv4, gates, gm = _inproj(xs, l, mix_pre, w_proj, b_proj, w_gate, b_gate, w_gm, b_gm,
                                               gmlp_ws, ln_g, ln_b, bsx)
        pm = _pool(a, l, pool_w_b, pool_scale_v, batch, seq)
        cmp = _cmpkv(kvc, l, cmp_pos, cmp_w1, cmp_w2, batch, seq)
        o_nsa = _nsa(q_hm, cmp, kv4, gates, ov, ex, batch, seq)
        xs = _merge(xs, l, pm, o_nsa, gm, mix_pre, w_mg, b_mg, w_br_pool_b, w_br_nsa_b, w_br_gmlp_b,
                    w_mix_out_b, mix_post)
        xs = _memattn(xs, l, mem_pre, mem_wq_b, mem, mem_kv, mem_wk_b, mem_wv_b, mem_wo_b, mem_post, batch, seq)
        xs = _ffn(xs, l, *ff2)
    return xs.reshape(batch, seq, d)
```
